```python
import jax, jax.numpy as jnp
from jax import lax
import numpy as np

D_MODEL = 1024
BATCH = 4
SEQ = 8192
DEPTH = 2

HG_WIDTH = D_MODEL // 2
HG_DK = 128
HG_HEADS = HG_WIDTH // HG_DK
HG_CHUNK = 64
RW_WIDTH = D_MODEL // 2
RW_HEAD = 64
RW_HEADS = RW_WIDTH // RW_HEAD
RW_DECAY_RANK = 64
RW_ICL_RANK = 64
RW_VRES_RANK = 32
N_BRANCH = 2
HG_COLS = 4 * HG_WIDTH
RW_COLS = 4 * RW_WIDTH + RW_DECAY_RANK + RW_ICL_RANK
GATE_COLS = N_BRANCH * D_MODEL
N_IN = HG_COLS + RW_COLS + GATE_COLS
DN_ALPHA = (2 * DEPTH) ** 0.25
DN_BETA = (8 * DEPTH) ** -0.25
LN_EPS = 1e-5
RMS_EPS = 1e-6
GN_EPS = 64e-5
L2_EPS = 1e-12
LB_FLOOR = 1e-30

kernel_name = "hgrn2_rwkv7_gated_hybrid"


def layer_norm(x, g, b):
    xf = x.astype(jnp.float32)
    mu = xf.mean(-1, keepdims=True)
    var = jnp.mean(jnp.square(xf - mu), -1, keepdims=True)
    return ((xf - mu) * lax.rsqrt(var + LN_EPS)).astype(x.dtype) * g + b


def heads(t, n):
    return t.reshape(t.shape[:-1] + (-1, n))


def token_shift(p, mu):
    prev = jnp.pad(p[:, :-1], ((0, 0), (1, 0), (0, 0)))
    return p + (prev - p) * mu


def hgrn2_chunked(q, k, v, log_f):
    B, T, H, K = q.shape
    V = v.shape[-1]
    C = HG_CHUNK
    N = T // C

    def to_chunks(a):
        return a.reshape(B, N, C, H, a.shape[-1]).transpose(1, 0, 3, 2, 4)

    qc, kc, vc, gc = (to_chunks(a) for a in (q, k, v, log_f))
    causal = jnp.tril(jnp.ones((C, C), dtype=bool))[:, :, None]

    def step(S, inp):
        qi, ki, vi, gi = inp
        b = jnp.cumsum(gi, axis=2)
        diff = b[:, :, :, None, :] - b[:, :, None, :, :]
        decay = jnp.where(causal, jnp.exp(jnp.where(causal, diff, 0.0)), 0.0)
        scores = jnp.einsum('bhtk,bhtsk,bhsk->bhts', qi, decay, ki)
        o = (jnp.einsum('bhts,bhsv->bhtv', scores, vi)
             + jnp.einsum('bhtk,bhkv->bhtv', qi * jnp.exp(b), S))
        b_last = b[:, :, -1:, :]
        S = (jnp.exp(b_last[:, :, 0, :])[..., None] * S
             + jnp.einsum('bhsk,bhsv->bhkv', ki * jnp.exp(b_last - b), vi))
        return S, o

    S0 = jnp.zeros((B, H, K, V), q.dtype)
    _, o = lax.scan(step, S0, (qc, kc, vc, gc))
    return o.transpose(1, 0, 3, 2, 4).reshape(B, T, H, V)


def rwkv7_scan(r, w, k, v, kk, a):
    B, T, H, N = r.shape

    def step(S, inp):
        r_t, w_t, k_t, v_t, kk_t, a_t = inp
        sa = jnp.einsum('bhvk,bhk->bhv', S, -kk_t)
        S = (S * w_t[:, :, None, :] + sa[..., None] * (kk_t * a_t)[:, :, None, :]
             + v_t[..., None] * k_t[:, :, None, :])
        y = jnp.einsum('bhvk,bhk->bhv', S, r_t)
        return S, y

    xs = tuple(jnp.moveaxis(t, 1, 0) for t in (r, w, k, v, kk, a))
    S0 = jnp.zeros((B, H, N, N), r.dtype)
    _, y = lax.scan(step, S0, xs)
    return jnp.moveaxis(y, 0, 1)


def setup_inputs(seed: int = 0) -> dict:
    key = jax.random.key(seed)
    ks = iter(jax.random.split(key, 32))
    nrm = lambda shape, s: jax.random.normal(next(ks), shape, jnp.float32) * s
    uni = lambda shape, lo, hi: jax.random.uniform(next(ks), shape, jnp.float32, lo, hi)
    return {
        "x": nrm((BATCH, SEQ, D_MODEL), 1.0),
        "c": nrm((BATCH, D_MODEL), 1.0),
        "w_ada": nrm((DEPTH, D_MODEL, 3 * D_MODEL), 0.1 * D_MODEL ** -0.5),
        "b_ada": nrm((DEPTH, 3 * D_MODEL), 0.01),
        "w_in": nrm((DEPTH, D_MODEL, N_IN), D_MODEL ** -0.5),
        "hg_lower_bounds": nrm((DEPTH, HG_WIDTH), 0.5),
        "hg_norm_g": 1.0 + nrm((DEPTH, HG_WIDTH), 0.01),
        "rw_mu": uni((DEPTH, RW_COLS), 0.0, 1.0),
        "rw_w0": uni((DEPTH, RW_WIDTH), -6.0, 0.0),
        "rw_w_up": nrm((DEPTH, RW_DECAY_RANK, RW_WIDTH), 0.1 * RW_DECAY_RANK ** -0.5),
        "rw_a0": nrm((DEPTH, RW_WIDTH), 0.1),
        "rw_a_up": nrm((DEPTH, RW_ICL_RANK, RW_WIDTH), 0.1 * RW_ICL_RANK ** -0.5),
        "rw_k_k": 0.85 + nrm((DEPTH, RW_WIDTH), 0.02),
        "rw_k_a": 1.0 + nrm((DEPTH, RW_WIDTH), 0.02),
        "rw_r_k": nrm((DEPTH, RW_HEADS, RW_HEAD), 0.1),
        "rw_v0": 1.0 + nrm((DEPTH - 1, RW_WIDTH), 0.1),
        "rw_v_down": nrm((DEPTH - 1, D_MODEL, RW_VRES_RANK), D_MODEL ** -0.5),
        "rw_v_up": nrm((DEPTH - 1, RW_VRES_RANK, RW_WIDTH), 0.1 * RW_VRES_RANK ** -0.5),
        "rw_gn_g": 1.0 + nrm((DEPTH, RW_WIDTH), 0.01),
        "rw_gn_b": nrm((DEPTH, RW_WIDTH), 0.01),
        "w_branch_hg": nrm((DEPTH, HG_WIDTH, D_MODEL), DN_BETA * HG_WIDTH ** -0.5),
        "w_branch_rw": nrm((DEPTH, RW_WIDTH, D_MODEL), DN_BETA * RW_WIDTH ** -0.5),
        "w_out": nrm((DEPTH, D_MODEL, D_MODEL), DN_BETA * D_MODEL ** -0.5),
        "ln_g": 1.0 + nrm((DEPTH, D_MODEL), 0.01),
        "ln_b": nrm((DEPTH, D_MODEL), 0.01),
    }


def reference(x, c, w_ada, b_ada, w_in, hg_lower_bounds, hg_norm_g, rw_mu, rw_w0, rw_w_up,
              rw_a0, rw_a_up, rw_k_k, rw_k_a, rw_r_k, rw_v0, rw_v_down, rw_v_up, rw_gn_g, rw_gn_b,
              w_branch_hg, w_branch_rw, w_out, ln_g, ln_b):
    f32 = jnp.float32
    B, T, _ = x.shape
    lb_soft = jax.nn.softmax(hg_lower_bounds.astype(f32), axis=0)
    lower_bounds = jnp.cumsum(lb_soft, axis=0) - lb_soft[0]
    v_first = None
    for l in range(DEPTH):
        cond = jax.nn.silu(c) @ w_ada[l] + b_ada[l]
        shift, scale, gate_raw = jnp.split(cond, 3, axis=-1)
        h = x * (1.0 + scale[:, None]) + shift[:, None]
        proj = h @ w_in[l]
        p_hg, p_rw, p_gate = jnp.split(proj, [HG_COLS, HG_COLS + RW_COLS], axis=-1)

        p_hg = p_hg.astype(f32)
        q, f_pre, i_val, z_hg = jnp.split(p_hg, 4, axis=-1)
        lb = lower_bounds[l]
        log_f = jnp.logaddexp(jnp.log(jnp.maximum(lb, LB_FLOOR)),
                              jnp.log1p(-lb) + jax.nn.log_sigmoid(f_pre))
        k_hg = (1.0 - lb) * jax.nn.sigmoid(-f_pre)
        o_hg = hgrn2_chunked(heads(q, HG_DK), heads(k_hg, HG_DK), heads(i_val, HG_DK), heads(log_f, HG_DK))
        o_hg = o_hg * lax.rsqrt(jnp.mean(jnp.square(o_hg), -1, keepdims=True) + RMS_EPS)
        o_hg = o_hg.reshape(B, T, HG_WIDTH) * hg_norm_g[l] * jax.nn.silu(z_hg)
        y_hg = o_hg.astype(x.dtype) @ w_branch_hg[l]

        p_rw = token_shift(p_rw.astype(f32), rw_mu[l])
        r, k, v, z_rw, w_down, a_down = jnp.split(
            p_rw, [RW_WIDTH, 2 * RW_WIDTH, 3 * RW_WIDTH, 4 * RW_WIDTH, 4 * RW_WIDTH + RW_DECAY_RANK], axis=-1)
        w_raw = -jax.nn.softplus(-(rw_w0[l] + jnp.tanh(w_down) @ rw_w_up[l])) - 0.5
        decay = jnp.exp(-jnp.exp(w_raw))
        if l == 0:
            v_first = v
        else:
            v_mix = jax.nn.sigmoid(rw_v0[l - 1] + (h.astype(f32) @ rw_v_down[l - 1]) @ rw_v_up[l - 1])
            v = v + (v_first - v) * v_mix
        a = jax.nn.sigmoid(rw_a0[l] + a_down @ rw_a_up[l])
        kk = heads(k * rw_k_k[l], RW_HEAD)
        kk = kk / jnp.maximum(jnp.sqrt(jnp.sum(jnp.square(kk), -1, keepdims=True)), L2_EPS)
        k = k * (1.0 + (a - 1.0) * rw_k_a[l])
        r_h, k_h, v_h = heads(r, RW_HEAD), heads(k, RW_HEAD), heads(v, RW_HEAD)
        y_rw = rwkv7_scan(r_h, heads(decay, RW_HEAD), k_h, v_h, kk, heads(a, RW_HEAD))
        mu = y_rw.mean(-1, keepdims=True)
        var = jnp.mean(jnp.square(y_rw - mu), -1, keepdims=True)
        y_rw = ((y_rw - mu) * lax.rsqrt(var + GN_EPS)).reshape(B, T, RW_WIDTH) * rw_gn_g[l] + rw_gn_b[l]
        bonus = jnp.sum(r_h * k_h * rw_r_k[l], -1, keepdims=True) * v_h
        y_rw = (y_rw + bonus.reshape(B, T, RW_WIDTH)) * jax.nn.silu(z_rw)
        y_rw = y_rw.astype(x.dtype) @ w_branch_rw[l]

        g_hg, g_rw = jnp.split(p_gate, N_BRANCH, axis=-1)
        merged = jax.nn.sigmoid(g_hg) * y_hg + jax.nn.sigmoid(g_rw) * y_rw
        out = merged @ w_out[l]
        x = layer_norm(DN_ALPHA * x + (1.0 + gate_raw[:, None]) * out, ln_g[l], ln_b[l])
    return x
```

```python
import functools
import math

import numpy as np
import jax
import jax.numpy as jnp
from jax import lax
from jax.experimental import pallas as pl
from jax.experimental.pallas import tpu as pltpu

F32 = jnp.float32
BF16 = jnp.bfloat16

HG_DK = 128
RW_HEAD = 64
LN_EPS = 1e-5
RMS_EPS = 1e-6
GN_EPS = 64e-5
L2_EPS = 1e-12
LB_FLOOR = 1e-30

LANES = 128
CHUNK = 128
LEVELS = int(math.log2(CHUNK))
VMEM_LIMIT = 56 * 1024 * 1024

NT = (((1,), (1,)), ((), ()))
TN = (((0,), (0,)), ((), ()))
NN = (((1,), (0,)), ((), ()))


def _dot(a, b, dims=NN):
    return lax.dot_general(a.astype(BF16), b.astype(BF16), dims, preferred_element_type=F32)


def _split2(x):
    hi = x.astype(BF16)
    lo = (x - hi.astype(F32)).astype(BF16)
    return hi, lo


def _split3(x):
    hi = x.astype(BF16)
    r1 = x - hi.astype(F32)
    mid = r1.astype(BF16)
    lo = (r1 - mid.astype(F32)).astype(BF16)
    return hi, mid, lo


def _dot3(a, b):
    ah, al = _split2(a)
    bh, bl = _split2(b)
    d = functools.partial(lax.dot_general, dimension_numbers=NN, preferred_element_type=F32)
    return d(ah, bh) + (d(ah, bl) + d(al, bh))


def _sel_dot(m01, x, pieces):
    parts = _split3(x) if pieces == 3 else _split2(x)
    w = x.shape[-1]
    cat = jnp.concatenate(parts, axis=-1)
    out = lax.dot_general(m01, cat, NN, preferred_element_type=F32)
    acc = out[:, 0:w]
    for i in range(1, pieces):
        acc = acc + out[:, i * w:(i + 1) * w]
    return acc


def _sigmoid(x):
    return 1.0 / (1.0 + jnp.exp(-x))


def _silu(x):
    return x * _sigmoid(x)


def _softplus(x):
    return jnp.maximum(x, 0.0) + jnp.log1p(jnp.exp(-jnp.abs(x)))


def _split_code(c):
    t = np.arange(c)[:, None]
    s = np.arange(c)[None, :]
    x = t ^ s
    code = np.where(x > 0, np.floor(np.log2(np.maximum(x, 1))).astype(np.int32), -1)
    code = np.where(s > t, -2, code)
    return code.astype(np.int32)


def _hgrn_segment_matrix(c):
    levels = int(math.log2(c))
    blocks = []
    j = np.arange(c)[None, :]
    t = np.arange(c)[:, None]
    for l in range(levels):
        m = 1 << l
        start = (t // (2 * m)) * (2 * m)
        mid = start + m - 1
        second = ((t // m) % 2) == 1
        blk = np.where(second, (j > mid) & (j <= t), (j > t) & (j <= mid))
        blocks.append(blk)
    blocks.append(j <= t)
    blocks.append(j > t)
    return np.concatenate(blocks, axis=0).astype(np.float32)


def _block_ones(n, blk):
    i = np.arange(n)
    return (i[:, None] // blk == i[None, :] // blk).astype(np.float32)


def _ada_kernel(c_ref, w_ref, b_ref, o_ref):
    sc = _silu(c_ref[...])
    o_ref[0] = _dot3(sc, w_ref[0]) + b_ref[0]


def _ada_call(c_pad, w_ada, b_ada):
    depth, d, d3 = w_ada.shape
    rows = c_pad.shape[0]
    tn = d
    return pl.pallas_call(
        _ada_kernel,
        grid=(depth, d3 // tn),
        in_specs=[
            pl.BlockSpec((rows, d), lambda l, j: (0, 0)),
            pl.BlockSpec((1, d, tn), lambda l, j: (l, 0, j)),
            pl.BlockSpec((1, 1, tn), lambda l, j: (l, 0, j)),
        ],
        out_specs=pl.BlockSpec((1, rows, tn), lambda l, j: (l, 0, j)),
        out_shape=jax.ShapeDtypeStruct((depth, rows, d3), F32),
        compiler_params=pltpu.CompilerParams(
            dimension_semantics=("parallel", "parallel"), vmem_limit_bytes=VMEM_LIMIT),
        name="adaln_cond",
    )(c_pad, w_ada, b_ada.reshape(depth, 1, d3))


def _proj_kernel(x_ref, shift_ref, scale_ref, w_ref, o_ref, h_ref):
    @pl.when(pl.program_id(1) == 0)
    def _():
        h = x_ref[...] * (1.0 + scale_ref[0]) + shift_ref[0]
        h_ref[...] = h.astype(BF16)

    o_ref[...] = lax.dot_general(h_ref[...], w_ref[...], NN, preferred_element_type=F32)


def _proj_call(x2, shift, scale, w_cat, seq, tm, tn):
    bt, d = x2.shape
    n = w_cat.shape[1]
    per_b = seq // tm
    return pl.pallas_call(
        _proj_kernel,
        grid=(bt // tm, n // tn),
        in_specs=[
            pl.BlockSpec((tm, d), lambda i, j: (i, 0)),
            pl.BlockSpec((1, 1, d), lambda i, j: (i // per_b, 0, 0)),
            pl.BlockSpec((1, 1, d), lambda i, j: (i // per_b, 0, 0)),
            pl.BlockSpec((d, tn), lambda i, j: (0, j)),
        ],
        out_specs=pl.BlockSpec((tm, tn), lambda i, j: (i, j)),
        out_shape=jax.ShapeDtypeStruct((bt, n), F32),
        scratch_shapes=[pltpu.VMEM((tm, d), BF16)],
        compiler_params=pltpu.CompilerParams(
            dimension_semantics=("parallel", "arbitrary"), vmem_limit_bytes=VMEM_LIMIT),
        name="modulate_in_proj",
    )(x2, shift, scale, w_cat)


def _hgrn_kernel(q_ref, f_ref, i_ref, z_ref, lb_ref, g_ref, seg_ref, code_ref, o_ref,
                 st_ref, lf_ref, k_ref, oacc_ref, *, n_heads, n_chunks):
    c = CHUNK

    @pl.when(pl.program_id(1) == 0)
    def _():
        st_ref[...] = jnp.zeros_like(st_ref)

    lb = lb_ref[...]
    f_pre = f_ref[...]
    log_lb = jnp.log(jnp.maximum(lb, LB_FLOOR))
    y = jnp.log1p(-lb) + (-_softplus(-f_pre))
    mx = jnp.maximum(log_lb, y)
    lf_ref[...] = mx + jnp.log1p(jnp.exp(-jnp.abs(log_lb - y)))
    k_ref[...] = (1.0 - lb) * _sigmoid(-f_pre)

    code = code_ref[...]
    seg = seg_ref[...]

    def chunk_body(ci, carry):
        rows = pl.ds(pl.multiple_of(ci * c, c), c)
        neg_g = -lf_ref[rows, :]
        d_all = _sel_dot(seg, neg_g, 2)
        b_cum = -d_all[LEVELS * c:(LEVELS + 1) * c, :]
        d_suf = d_all[(LEVELS + 1) * c:(LEVELS + 2) * c, :]
        e_b = jnp.exp(b_cum)
        e_suf = jnp.exp(-d_suf)
        e_last = e_b[c - 1:c, :]
        q = q_ref[rows, :]
        k = k_ref[rows, :]
        v = i_ref[rows, :]
        for h in range(n_heads):
            sl = slice(h * HG_DK, (h + 1) * HG_DK)
            qh, kh, vh = q[:, sl], k[:, sl], v[:, sl]
            sc = jnp.where(code == -1, _dot(qh, kh, NT), 0.0)
            for l in range(LEVELS):
                e = jnp.exp(-d_all[l * c:(l + 1) * c, sl])
                sc = jnp.where(code == l, _dot(qh * e, kh * e, NT), sc)
            st = st_ref[h]
            o = _dot(sc, vh) + _dot(qh * e_b[:, sl], st, NT)
            st_ref[h] = st * e_last[:, sl] + _dot(vh, kh * e_suf[:, sl], TN)
            oacc_ref[rows, sl] = o
        return carry

    lax.fori_loop(0, n_chunks, chunk_body, 0)

    g = g_ref[...]
    z = z_ref[...]
    for h in range(n_heads):
        sl = slice(h * HG_DK, (h + 1) * HG_DK)
        o = oacc_ref[:, sl]
        o = o * lax.rsqrt(jnp.mean(o * o, axis=-1, keepdims=True) + RMS_EPS)
        o_ref[:, sl] = o * g[:, sl] * _silu(z[:, sl])


def _hgrn_call(proj, lb, norm_g, seg, code, batch, seq, ct):
    w = lb.shape[-1]
    n_heads = w // HG_DK
    nt = seq // ct
    col = lambda j: (lambda b, t: (b * nt + t, j))
    const = lambda b, t: (0, 0)
    kern = functools.partial(_hgrn_kernel, n_heads=n_heads, n_chunks=ct // CHUNK)
    return pl.pallas_call(
        kern,
        grid=(batch, nt),
        in_specs=[
            pl.BlockSpec((ct, w), col(0)),
            pl.BlockSpec((ct, w), col(1)),
            pl.BlockSpec((ct, w), col(2)),
            pl.BlockSpec((ct, w), col(3)),
            pl.BlockSpec((1, w), const),
            pl.BlockSpec((1, w), const),
            pl.BlockSpec(seg.shape, const),
            pl.BlockSpec(code.shape, const),
        ],
        out_specs=pl.BlockSpec((ct, w), lambda b, t: (b * nt + t, 0)),
        out_shape=jax.ShapeDtypeStruct((batch * seq, w), F32),
        scratch_shapes=[
            pltpu.VMEM((n_heads, HG_DK, HG_DK), F32),
            pltpu.VMEM((ct, w), F32),
            pltpu.VMEM((ct, w), F32),
            pltpu.VMEM((ct, w), F32),
        ],
        compiler_params=pltpu.CompilerParams(
            dimension_semantics=("parallel", "arbitrary"), vmem_limit_bytes=VMEM_LIMIT),
        name="hgrn2_branch",
    )(proj, proj, proj, proj, lb, norm_g, seg, code)


def _unit_lower_inverse(low, code):
    c = low.shape[0]
    eye = jnp.where(code == -1, 1.0, 0.0)
    t = eye - jnp.where(code == 0, low, 0.0)
    for l in range(1, int(math.log2(c))):
        off = jnp.where(code == l, low, 0.0)
        t = t - _dot3(_dot3(t, off), t)
    return t


def _rwkv_kernel(*refs, n_chunks, first_layer):
    (r_ref, k_ref, v_ref, z_ref, lo_ref, mu_ref, mulo_ref, par_ref, wup_ref, aup_ref, vup_ref,
     bd_ref, tri_ref, code_ref) = refs[:14]
    if first_layer:
        o_ref, vout_ref = refs[14:16]
        scratch = refs[16:]
        vfirst_ref = None
    else:
        vfirst_ref = refs[14]
        o_ref = refs[15]
        vout_ref = None
        scratch = refs[16:]
    (prev_ref, prevlo_ref, st_ref, lw_ref, rr_ref, kk_ref, kn_ref, be_ref, vv_ref, y_ref) = scratch
    c = CHUNK
    ct = r_ref.shape[0]
    w = r_ref.shape[1]
    n_pairs = w // LANES

    @pl.when(pl.program_id(1) == 0)
    def _():
        prev_ref[...] = jnp.zeros_like(prev_ref)
        prevlo_ref[...] = jnp.zeros_like(prevlo_ref)
        st_ref[...] = jnp.zeros_like(st_ref)

    row0 = lax.broadcasted_iota(jnp.int32, (ct, 1), 0) == 0

    def shifted(p, prev_row, mu):
        prev = jnp.where(row0, prev_row, pltpu.roll(p, 1, 0))
        return p + (prev - p) * mu

    p_r, p_k, p_v, p_z, p_lo = r_ref[...], k_ref[...], v_ref[...], z_ref[...], lo_ref[...]
    mu = mu_ref[...]
    r = shifted(p_r, prev_ref[0:1, :], mu[0:1, :])
    k = shifted(p_k, prev_ref[1:2, :], mu[1:2, :])
    v = shifted(p_v, prev_ref[2:3, :], mu[2:3, :])
    z = shifted(p_z, prev_ref[3:4, :], mu[3:4, :])
    lo = shifted(p_lo, prevlo_ref[0:1, :], mulo_ref[...])
    prev_ref[0:1, :] = p_r[ct - 1:ct, :]
    prev_ref[1:2, :] = p_k[ct - 1:ct, :]
    prev_ref[2:3, :] = p_v[ct - 1:ct, :]
    prev_ref[3:4, :] = p_z[ct - 1:ct, :]
    prevlo_ref[0:1, :] = p_lo[ct - 1:ct, :]

    par = par_ref[...]
    w0, a0, k_k, k_a, r_k, gn_g, gn_b, v0 = (par[i:i + 1, :] for i in range(8))
    bd = bd_ref[...]
    half = bd.shape[0]

    def head_sum(x):
        outs = []
        for j in range(w // half):
            xh, xl = _split2(x[:, j * half:(j + 1) * half])
            d = functools.partial(lax.dot_general, dimension_numbers=NN, preferred_element_type=F32)
            outs.append(d(xh, bd) + d(xl, bd))
        return jnp.concatenate(outs, axis=-1)

    w_raw = -_softplus(-(w0 + _dot(jnp.tanh(lo), wup_ref[...]))) - 0.5
    lw_ref[...] = -jnp.exp(w_raw)
    a = _sigmoid(a0 + _dot(lo, aup_ref[...]))
    if first_layer:
        vout_ref[...] = v
    else:
        v_mix = _sigmoid(v0 + _dot(lo, vup_ref[...]))
        v = v + (vfirst_ref[...] - v) * v_mix
    kk = k * k_k
    kk = kk / jnp.maximum(jnp.sqrt(head_sum(kk * kk)), L2_EPS)
    k = k * (1.0 + (a - 1.0) * k_a)
    bonus = head_sum(r * k * r_k) * v

    rr_ref[...] = r
    kn_ref[...] = kk
    kk_ref[...] = k
    be_ref[...] = kk * a
    vv_ref[...] = v

    code = code_ref[...]
    tri = tri_ref[...]
    lane = lax.broadcasted_iota(jnp.int32, (1, LANES), 1)
    m_a = lane < RW_HEAD
    row_i = lax.broadcasted_iota(jnp.int32, (LANES, LANES), 0)
    col_i = lax.broadcasted_iota(jnp.int32, (LANES, LANES), 1)
    same_head = (row_i < RW_HEAD) == (col_i < RW_HEAD)
    strict = code >= 0
    incl = code >= -1

    def chunk_body(ci, carry):
        rows = pl.ds(pl.multiple_of(ci * c, c), c)
        lw = lw_ref[rows, :]
        cum = _sel_dot(tri, lw, 3)
        cum_prev = cum - lw
        c_mid = cum[c // 2 - 1:c // 2, :]
        c_end = cum[c - 1:c, :]
        rc, kc, knc, bc, vc = rr_ref[rows, :], kk_ref[rows, :], kn_ref[rows, :], be_ref[rows, :], vv_ref[rows, :]
        e_cum = jnp.exp(cum)
        e_prev = jnp.exp(cum_prev)
        e_mid_fwd = jnp.exp(cum - c_mid)
        e_mid_prev = jnp.exp(cum_prev - c_mid)
        e_mid_bwd = jnp.exp(c_mid - cum)
        e_end = jnp.exp(c_end - cum)
        r_abs, kn_abs = rc * e_cum, knc * e_prev
        r_mid, kn_mid = rc * e_mid_fwd, knc * e_mid_prev
        k_mid, b_mid = kc * e_mid_bwd, bc * e_mid_bwd
        k_end, b_end = kc * e_end, bc * e_end
        w_end = jnp.exp(c_end)
        for p in range(n_pairs):
            sl = slice(p * LANES, (p + 1) * LANES)
            knm, rm = kn_mid[:, sl], r_mid[:, sl]
            lhs = jnp.concatenate([jnp.where(m_a, knm, 0.0), jnp.where(m_a, 0.0, knm),
                                   jnp.where(m_a, rm, 0.0), jnp.where(m_a, 0.0, rm)], axis=0)
            rhs = jnp.concatenate([k_mid[:, sl], b_mid[:, sl]], axis=0)
            g = _dot(lhs, rhs, NT)
            vp = vc[:, sl]
            st = st_ref[p]
            x1 = _dot(kn_abs[:, sl], st, NT)
            a1v, tinv, a34 = [], [], []
            for hh in range(2):
                a1 = jnp.where(strict, g[hh * c:(hh + 1) * c, 0:c], 0.0)
                low = jnp.where(strict, g[hh * c:(hh + 1) * c, c:2 * c], 0.0)
                a3 = jnp.where(incl, g[(2 + hh) * c:(3 + hh) * c, 0:c], 0.0)
                a4 = jnp.where(incl, g[(2 + hh) * c:(3 + hh) * c, c:2 * c], 0.0)
                a1v.append(_dot(a1, vp))
                tinv.append(_unit_lower_inverse(low, code))
                a34.append(jnp.concatenate([a3, a4], axis=1))
            x1 = x1 + jnp.where(m_a, a1v[0], a1v[1])
            sa = -jnp.where(m_a, _dot(tinv[0], x1), _dot(tinv[1], x1))
            vs = jnp.concatenate([vp, sa], axis=0)
            y = _dot(r_abs[:, sl], st, NT) + jnp.where(m_a, _dot(a34[0], vs), _dot(a34[1], vs))
            zed = jnp.concatenate([k_end[:, sl], b_end[:, sl]], axis=0)
            st_ref[p] = st * w_end[:, sl] + jnp.where(same_head, _dot(vs, zed, TN), 0.0)
            y_ref[rows, sl] = y
        return carry

    lax.fori_loop(0, n_chunks, chunk_body, 0)

    y = y_ref[...]
    inv_n = 1.0 / RW_HEAD
    mean = head_sum(y) * inv_n
    yc = y - mean
    var = head_sum(yc * yc) * inv_n
    yn = yc * lax.rsqrt(var + GN_EPS) * gn_g + gn_b
    o_ref[...] = (yn + bonus) * _silu(z)


def _rwkv_call(proj, mu4, mu_lo, par, wup, aup, vup, bd, tri, code, v_first, batch, seq, ct, col0):
    w = par.shape[-1]
    wl = mu_lo.shape[-1]
    nt = seq // ct
    first_layer = v_first is None
    col = lambda j: (lambda b, t: (b * nt + t, j))
    const = lambda b, t: (0, 0)
    row_spec = pl.BlockSpec((ct, w), lambda b, t: (b * nt + t, 0))
    in_specs = [
        pl.BlockSpec((ct, w), col(col0)),
        pl.BlockSpec((ct, w), col(col0 + 1)),
        pl.BlockSpec((ct, w), col(col0 + 2)),
        pl.BlockSpec((ct, w), col(col0 + 3)),
        pl.BlockSpec((ct, wl), col((col0 + 4) * (w // wl))),
        pl.BlockSpec(mu4.shape, const),
        pl.BlockSpec(mu_lo.shape, const),
        pl.BlockSpec(par.shape, const),
        pl.BlockSpec(wup.shape, const),
        pl.BlockSpec(aup.shape, const),
        pl.BlockSpec(vup.shape, const),
        pl.BlockSpec(bd.shape, const),
        pl.BlockSpec(tri.shape, const),
        pl.BlockSpec(code.shape, const),
    ]
    args = [proj, proj, proj, proj, proj, mu4, mu_lo, par, wup, aup, vup, bd, tri, code]
    out_sds = jax.ShapeDtypeStruct((batch * seq, w), F32)
    if first_layer:
        out_specs = [row_spec, row_spec]
        out_shape = [out_sds, out_sds]
    else:
        in_specs.append(row_spec)
        args.append(v_first)
        out_specs = row_spec
        out_shape = out_sds
    kern = functools.partial(_rwkv_kernel, n_chunks=ct // CHUNK, first_layer=first_layer)
    return pl.pallas_call(
        kern,
        grid=(batch, nt),
        in_specs=in_specs,
        out_specs=out_specs,
        out_shape=out_shape,
        scratch_shapes=[
            pltpu.VMEM((8, w), F32),
            pltpu.VMEM((8, wl), F32),
            pltpu.VMEM((w // LANES, LANES, LANES), F32),
            pltpu.VMEM((ct, w), F32),
            pltpu.VMEM((ct, w), F32),
            pltpu.VMEM((ct, w), F32),
            pltpu.VMEM((ct, w), F32),
            pltpu.VMEM((ct, w), F32),
            pltpu.VMEM((ct, w), F32),
            pltpu.VMEM((ct, w), F32),
        ],
        compiler_params=pltpu.CompilerParams(
            dimension_semantics=("parallel", "arbitrary"), vmem_limit_bytes=VMEM_LIMIT),
        name="rwkv7_branch_first" if first_layer else "rwkv7_branch",
    )(*args)


def _out_kernel(x_ref, ohg_ref, orw_ref, ghg_ref, grw_ref, gate_ref, wa_ref, wb_ref, wo_ref,
                lng_ref, lnb_ref, o_ref, *, alpha):
    d = functools.partial(lax.dot_general, dimension_numbers=NN, preferred_element_type=F32)
    y_hg = d(ohg_ref[...].astype(BF16), wa_ref[...])
    y_rw = d(orw_ref[...].astype(BF16), wb_ref[...])
    merged = _sigmoid(ghg_ref[...]) * y_hg + _sigmoid(grw_ref[...]) * y_rw
    out = d(merged.astype(BF16), wo_ref[...])
    xn = alpha * x_ref[...] + (1.0 + gate_ref[0]) * out
    mu = jnp.mean(xn, axis=-1, keepdims=True)
    xc = xn - mu
    var = jnp.mean(xc * xc, axis=-1, keepdims=True)
    o_ref[...] = xc * lax.rsqrt(var + LN_EPS) * lng_ref[...] + lnb_ref[...]


def _out_call(x2, o_hg, o_rw, proj, gate, wa, wb, wo, ln_g, ln_b, seq, tm, gate_col, alpha):
    bt, d = x2.shape
    wh = o_hg.shape[1]
    per_b = seq // tm
    row = lambda i: (i, 0)
    const = lambda i: (0, 0)
    return pl.pallas_call(
        functools.partial(_out_kernel, alpha=alpha),
        grid=(bt // tm,),
        in_specs=[
            pl.BlockSpec((tm, d), row),
            pl.BlockSpec((tm, wh), row),
            pl.BlockSpec((tm, wh), row),
            pl.BlockSpec((tm, d), lambda i: (i, gate_col)),
            pl.BlockSpec((tm, d), lambda i: (i, gate_col + 1)),
            pl.BlockSpec((1, 1, d), lambda i: (i // per_b, 0, 0)),
            pl.BlockSpec(wa.shape, const),
            pl.BlockSpec(wb.shape, const),
            pl.BlockSpec(wo.shape, const),
            pl.BlockSpec((1, d), const),
            pl.BlockSpec((1, d), const),
        ],
        out_specs=pl.BlockSpec((tm, d), row),
        out_shape=jax.ShapeDtypeStruct((bt, d), F32),
        compiler_params=pltpu.CompilerParams(
            dimension_semantics=("parallel",), vmem_limit_bytes=VMEM_LIMIT),
        name="merge_out_proj_norm",
    )(x2, o_hg, o_rw, proj, proj, gate, wa, wb, wo, ln_g, ln_b)


def kernel(x, c, w_ada, b_ada, w_in, hg_lower_bounds, hg_norm_g, rw_mu, rw_w0, rw_w_up, rw_a0, rw_a_up,
           rw_k_k, rw_k_a, rw_r_k, rw_v0, rw_v_down, rw_v_up, rw_gn_g, rw_gn_b, w_branch_hg, w_branch_rw,
           w_out, ln_g, ln_b):
    batch, seq, d = x.shape
    depth = w_in.shape[0]
    hgw = hg_norm_g.shape[1]
    rww = rw_w0.shape[1]
    r_decay = rw_w_up.shape[1]
    r_icl = rw_a_up.shape[1]
    r_vres = rw_v_up.shape[1]
    hg_cols = 4 * hgw
    rw_main = 4 * rww
    lo_w = 2 * LANES
    assert r_decay + r_icl + r_vres <= lo_w and hgw == rww and d == 2 * hgw
    assert seq % CHUNK == 0
    alpha = float((2 * depth) ** 0.25)

    lb_soft = jax.nn.softmax(hg_lower_bounds.astype(F32), axis=0)
    lower_bounds = jnp.cumsum(lb_soft, axis=0) - lb_soft[0]

    seg = jnp.asarray(_hgrn_segment_matrix(CHUNK), BF16)
    code = jnp.asarray(_split_code(CHUNK), jnp.int32)
    tri = jnp.asarray(np.tril(np.ones((CHUNK, CHUNK), np.float32)), BF16)
    bd = jnp.asarray(_block_ones(2 * LANES, RW_HEAD), BF16)

    c_pad = jnp.zeros((8, d), F32).at[:batch].set(c)
    cond = _ada_call(c_pad, w_ada, b_ada)

    x2 = x.reshape(batch * seq, d)
    tm = min(1024, seq)
    ct = min(512, seq)
    v_first = None
    for l in range(depth):
        shift = cond[l, :batch, 0:d].reshape(batch, 1, d)
        scale = cond[l, :batch, d:2 * d].reshape(batch, 1, d)
        gate = cond[l, :batch, 2 * d:3 * d].reshape(batch, 1, d)

        wl = w_in[l]
        w_hg = wl[:, :hg_cols]
        w_rw = wl[:, hg_cols:hg_cols + rw_main]
        w_lo = wl[:, hg_cols + rw_main:hg_cols + rw_main + r_decay + r_icl]
        w_gate = wl[:, hg_cols + rw_main + r_decay + r_icl:]
        w_vd = rw_v_down[l - 1] if l > 0 else jnp.zeros((d, r_vres), F32)
        pad = jnp.zeros((d, lo_w - (r_decay + r_icl + r_vres)), F32)
        w_cat = jnp.concatenate([w_hg, w_gate, w_rw, w_lo, w_vd, pad], axis=1).astype(BF16)
        n_cat = w_cat.shape[1]
        tn = n_cat // 5 if (n_cat // 5) % LANES == 0 and n_cat % 5 == 0 else lo_w
        proj = _proj_call(x2, shift, scale, w_cat, seq, tm, tn)

        o_hg = _hgrn_call(proj, lower_bounds[l].reshape(1, hgw), hg_norm_g[l].reshape(1, hgw),
                          seg, code, batch, seq, ct)

        mu_l = rw_mu[l]
        mu4 = jnp.zeros((8, rww), F32).at[0:4].set(mu_l[:rw_main].reshape(4, rww))
        mu_lo = jnp.zeros((1, lo_w), F32).at[0, :r_decay + r_icl].set(mu_l[rw_main:])
        v0 = rw_v0[l - 1] if l > 0 else jnp.zeros((rww,), F32)
        par = jnp.stack([rw_w0[l], rw_a0[l], rw_k_k[l], rw_k_a[l], rw_r_k[l].reshape(rww),
                         rw_gn_g[l], rw_gn_b[l], v0], axis=0)
        wup = jnp.zeros((lo_w, rww), F32).at[0:r_decay].set(rw_w_up[l]).astype(BF16)
        aup = jnp.zeros((lo_w, rww), F32).at[r_decay:r_decay + r_icl].set(rw_a_up[l]).astype(BF16)
        vup_l = rw_v_up[l - 1] if l > 0 else jnp.zeros((r_vres, rww), F32)
        vup = jnp.zeros((lo_w, rww), F32).at[r_decay + r_icl:r_decay + r_icl + r_vres].set(vup_l).astype(BF16)
        col0 = (hg_cols + w_gate.shape[1]) // rww
        res = _rwkv_call(proj, mu4, mu_lo, par, wup, aup, vup, bd, tri, code, v_first, batch, seq, ct, col0)
        if l == 0:
            o_rw, v_first = res
        else:
            o_rw = res

        x2 = _out_call(x2, o_hg, o_rw, proj, gate, w_branch_hg[l].astype(BF16), w_branch_rw[l].astype(BF16),
                       w_out[l].astype(BF16), ln_g[l].reshape(1, d), ln_b[l].reshape(1, d), seq,
                       min(512, seq), hg_cols // d, alpha)
    return x2.reshape(batch, seq, d)
```

```python
import functools
import math

import numpy as np
import jax
import jax.numpy as jnp
from jax import lax
from jax.experimental import pallas as pl
from jax.experimental.pallas import tpu as pltpu

F32 = jnp.float32
BF16 = jnp.bfloat16

HG_DK = 128
RW_HEAD = 64
LN_EPS = 1e-5
RMS_EPS = 1e-6
GN_EPS = 64e-5
L2_EPS = 1e-12
LB_FLOOR = 1e-30

LANES = 128
CHUNK = 128
LEVELS = int(math.log2(CHUNK))
VMEM_LIMIT = 56 * 1024 * 1024

NT = (((1,), (1,)), ((), ()))
TN = (((0,), (0,)), ((), ()))
NN = (((1,), (0,)), ((), ()))


def _dot(a, b, dims=NN):
    return lax.dot_general(a.astype(BF16), b.astype(BF16), dims, preferred_element_type=F32)


def _split2(x):
    hi = x.astype(BF16)
    lo = (x - hi.astype(F32)).astype(BF16)
    return hi, lo


def _split3(x):
    hi = x.astype(BF16)
    r1 = x - hi.astype(F32)
    mid = r1.astype(BF16)
    lo = (r1 - mid.astype(F32)).astype(BF16)
    return hi, mid, lo


def _dot3(a, b):
    ah, al = _split2(a)
    bh, bl = _split2(b)
    d = functools.partial(lax.dot_general, dimension_numbers=NN, preferred_element_type=F32)
    return d(ah, bh) + (d(ah, bl) + d(al, bh))


def _sel_dot(m01, x, pieces):
    parts = _split3(x) if pieces == 3 else _split2(x)
    w = x.shape[-1]
    cat = jnp.concatenate(parts, axis=-1)
    out = lax.dot_general(m01, cat, NN, preferred_element_type=F32)
    acc = out[:, 0:w]
    for i in range(1, pieces):
        acc = acc + out[:, i * w:(i + 1) * w]
    return acc


def _sigmoid(x):
    return 1.0 / (1.0 + jnp.exp(-x))


def _silu(x):
    return x * _sigmoid(x)


def _softplus(x):
    return jnp.maximum(x, 0.0) + jnp.log1p(jnp.exp(-jnp.abs(x)))


def _split_code(c):
    t = np.arange(c)[:, None]
    s = np.arange(c)[None, :]
    x = t ^ s
    code = np.where(x > 0, np.floor(np.log2(np.maximum(x, 1))).astype(np.int32), -1)
    code = np.where(s > t, -2, code)
    return code.astype(np.int32)


SUBLANES = 8
SMALL_LEVELS = int(math.log2(SUBLANES))


def _hgrn_segment_matrix(c):
    blocks = []
    j = np.arange(c)[None, :]
    t = np.arange(c)[:, None]
    for l in range(SMALL_LEVELS):
        m = 1 << l
        start = (t // (2 * m)) * (2 * m)
        mid = start + m - 1
        second = ((t // m) % 2) == 1
        blk = np.where(second, (j > mid) & (j <= t), (j > t) & (j <= mid))
        blocks.append(blk)
    blocks.append(j <= t)
    return np.concatenate(blocks, axis=0).astype(np.float32)


def _block_ones(n, blk):
    i = np.arange(n)
    return (i[:, None] // blk == i[None, :] // blk).astype(np.float32)


def _ada_kernel(c_ref, w_ref, b_ref, o_ref):
    sc = _silu(c_ref[...])
    o_ref[0] = _dot3(sc, w_ref[0]) + b_ref[0]


def _ada_call(c_pad, w_ada, b_ada):
    depth, d, d3 = w_ada.shape
    rows = c_pad.shape[0]
    tn = d
    return pl.pallas_call(
        _ada_kernel,
        grid=(depth, d3 // tn),
        in_specs=[
            pl.BlockSpec((rows, d), lambda l, j: (0, 0)),
            pl.BlockSpec((1, d, tn), lambda l, j: (l, 0, j)),
            pl.BlockSpec((1, 1, tn), lambda l, j: (l, 0, j)),
        ],
        out_specs=pl.BlockSpec((1, rows, tn), lambda l, j: (l, 0, j)),
        out_shape=jax.ShapeDtypeStruct((depth, rows, d3), F32),
        compiler_params=pltpu.CompilerParams(
            dimension_semantics=("parallel", "parallel"), vmem_limit_bytes=VMEM_LIMIT),
        name="adaln_cond",
    )(c_pad, w_ada, b_ada.reshape(depth, 1, d3))


def _proj_kernel(x_ref, shift_ref, scale_ref, w_ref, o_ref, h_ref):
    @pl.when(pl.program_id(1) == 0)
    def _():
        h = x_ref[...] * (1.0 + scale_ref[0]) + shift_ref[0]
        h_ref[...] = h.astype(BF16)

    o_ref[...] = lax.dot_general(h_ref[...], w_ref[...], NN, preferred_element_type=F32)


def _proj_call(x2, shift, scale, w_cat, seq, tm, tn):
    bt, d = x2.shape
    n = w_cat.shape[1]
    per_b = seq // tm
    return pl.pallas_call(
        _proj_kernel,
        grid=(bt // tm, n // tn),
        in_specs=[
            pl.BlockSpec((tm, d), lambda i, j: (i, 0)),
            pl.BlockSpec((1, 1, d), lambda i, j: (i // per_b, 0, 0)),
            pl.BlockSpec((1, 1, d), lambda i, j: (i // per_b, 0, 0)),
            pl.BlockSpec((d, tn), lambda i, j: (0, j)),
        ],
        out_specs=pl.BlockSpec((tm, tn), lambda i, j: (i, j)),
        out_shape=jax.ShapeDtypeStruct((bt, n), F32),
        scratch_shapes=[pltpu.VMEM((tm, d), BF16)],
        compiler_params=pltpu.CompilerParams(
            dimension_semantics=("parallel", "arbitrary"), vmem_limit_bytes=VMEM_LIMIT),
        name="modulate_in_proj",
    )(x2, shift, scale, w_cat)


def _hgrn_kernel(q_ref, f_ref, i_ref, z_ref, lb_ref, g_ref, seg_ref, code_ref, o_ref,
                 st_ref, lf_ref, k_ref, oacc_ref, *, n_heads, n_chunks):
    c = CHUNK

    @pl.when(pl.program_id(1) == 0)
    def _():
        st_ref[...] = jnp.zeros_like(st_ref)

    lb = lb_ref[...]
    f_pre = f_ref[...]
    log_lb = jnp.log(jnp.maximum(lb, LB_FLOOR))
    y = jnp.log1p(-lb) + (-_softplus(-f_pre))
    mx = jnp.maximum(log_lb, y)
    lf_ref[...] = mx + jnp.log1p(jnp.exp(-jnp.abs(log_lb - y)))
    k_ref[...] = (1.0 - lb) * _sigmoid(-f_pre)

    code = code_ref[...]
    seg = seg_ref[...]

    def chunk_body(ci, carry):
        rows = pl.ds(pl.multiple_of(ci * c, c), c)
        neg_g = -lf_ref[rows, :]
        d_all = _sel_dot(seg, neg_g, 2)
        nb = d_all[SMALL_LEVELS * c:(SMALL_LEVELS + 1) * c, :]
        e_b = jnp.exp(-nb)
        e_suf = jnp.exp(nb - nb[c - 1:c, :])
        e_last = e_b[c - 1:c, :]
        e_lvl = [jnp.exp(-d_all[l * c:(l + 1) * c, :]) for l in range(SMALL_LEVELS)]
        for l in range(SMALL_LEVELS, LEVELS):
            m = 1 << l
            ref = jnp.concatenate(
                [jnp.broadcast_to(nb[s + m - 1:s + m, :], (2 * m, nb.shape[1])) for s in range(0, c, 2 * m)], axis=0)
            e_lvl.append(jnp.exp(-jnp.abs(nb - ref)))
        q = q_ref[rows, :]
        k = k_ref[rows, :]
        v = i_ref[rows, :]
        for h in range(n_heads):
            sl = slice(h * HG_DK, (h + 1) * HG_DK)
            qh, kh, vh = q[:, sl], k[:, sl], v[:, sl]
            sc = jnp.where(code == -1, _dot(qh, kh, NT), 0.0)
            for l in range(LEVELS):
                e = e_lvl[l][:, sl]
                sc = jnp.where(code == l, _dot(qh * e, kh * e, NT), sc)
            st = st_ref[h]
            o = _dot(sc, vh) + _dot(qh * e_b[:, sl], st, NT)
            st_ref[h] = st * e_last[:, sl] + _dot(vh, kh * e_suf[:, sl], TN)
            oacc_ref[rows, sl] = o
        return carry

    lax.fori_loop(0, n_chunks, chunk_body, 0)

    g = g_ref[...]
    z = z_ref[...]
    for h in range(n_heads):
        sl = slice(h * HG_DK, (h + 1) * HG_DK)
        o = oacc_ref[:, sl]
        o = o * lax.rsqrt(jnp.mean(o * o, axis=-1, keepdims=True) + RMS_EPS)
        o_ref[:, sl] = o * g[:, sl] * _silu(z[:, sl])


def _hgrn_call(proj, lb, norm_g, seg, code, batch, seq, ct):
    w = lb.shape[-1]
    n_heads = w // HG_DK
    nt = seq // ct
    col = lambda j: (lambda b, t: (b * nt + t, j))
    const = lambda b, t: (0, 0)
    kern = functools.partial(_hgrn_kernel, n_heads=n_heads, n_chunks=ct // CHUNK)
    return pl.pallas_call(
        kern,
        grid=(batch, nt),
        in_specs=[
            pl.BlockSpec((ct, w), col(0)),
            pl.BlockSpec((ct, w), col(1)),
            pl.BlockSpec((ct, w), col(2)),
            pl.BlockSpec((ct, w), col(3)),
            pl.BlockSpec((1, w), const),
            pl.BlockSpec((1, w), const),
            pl.BlockSpec(seg.shape, const),
            pl.BlockSpec(code.shape, const),
        ],
        out_specs=pl.BlockSpec((ct, w), lambda b, t: (b * nt + t, 0)),
        out_shape=jax.ShapeDtypeStruct((batch * seq, w), F32),
        scratch_shapes=[
            pltpu.VMEM((n_heads, HG_DK, HG_DK), F32),
            pltpu.VMEM((ct, w), F32),
            pltpu.VMEM((ct, w), F32),
            pltpu.VMEM((ct, w), F32),
        ],
        compiler_params=pltpu.CompilerParams(
            dimension_semantics=("parallel", "arbitrary"), vmem_limit_bytes=VMEM_LIMIT),
        name="hgrn2_branch",
    )(proj, proj, proj, proj, lb, norm_g, seg, code)


def _rwkv_kernel(*refs, n_chunks, first_layer):
    (r_ref, k_ref, v_ref, z_ref, lo_ref, mu_ref, mulo_ref, par_ref, wup_ref, aup_ref, vup_ref,
     bd_ref, tri_ref, code_ref) = refs[:14]
    if first_layer:
        o_ref, vout_ref = refs[14:16]
        scratch = refs[16:]
        vfirst_ref = None
    else:
        vfirst_ref = refs[14]
        o_ref = refs[15]
        vout_ref = None
        scratch = refs[16:]
    (prev_ref, prevlo_ref, st_ref, lw_ref, rr_ref, kk_ref, kn_ref, be_ref, vv_ref, y_ref) = scratch
    c = CHUNK
    ct = r_ref.shape[0]
    w = r_ref.shape[1]
    n_pairs = w // LANES

    @pl.when(pl.program_id(1) == 0)
    def _():
        prev_ref[...] = jnp.zeros_like(prev_ref)
        prevlo_ref[...] = jnp.zeros_like(prevlo_ref)
        st_ref[...] = jnp.zeros_like(st_ref)

    row0 = lax.broadcasted_iota(jnp.int32, (ct, 1), 0) == 0

    def shifted(p, prev_row, mu):
        prev = jnp.where(row0, prev_row, pltpu.roll(p, 1, 0))
        return p + (prev - p) * mu

    p_r, p_k, p_v, p_z, p_lo = r_ref[...], k_ref[...], v_ref[...], z_ref[...], lo_ref[...]
    mu = mu_ref[...]
    r = shifted(p_r, prev_ref[0:1, :], mu[0:1, :])
    k = shifted(p_k, prev_ref[1:2, :], mu[1:2, :])
    v = shifted(p_v, prev_ref[2:3, :], mu[2:3, :])
    z = shifted(p_z, prev_ref[3:4, :], mu[3:4, :])
    lo = shifted(p_lo, prevlo_ref[0:1, :], mulo_ref[...])
    prev_ref[0:1, :] = p_r[ct - 1:ct, :]
    prev_ref[1:2, :] = p_k[ct - 1:ct, :]
    prev_ref[2:3, :] = p_v[ct - 1:ct, :]
    prev_ref[3:4, :] = p_z[ct - 1:ct, :]
    prevlo_ref[0:1, :] = p_lo[ct - 1:ct, :]

    par = par_ref[...]
    w0, a0, k_k, k_a, r_k, gn_g, gn_b, v0 = (par[i:i + 1, :] for i in range(8))
    bd = bd_ref[...]
    half = bd.shape[0]

    def head_sum(x):
        outs = []
        for j in range(w // half):
            xh, xl = _split2(x[:, j * half:(j + 1) * half])
            d = functools.partial(lax.dot_general, dimension_numbers=NN, preferred_element_type=F32)
            outs.append(d(xh, bd) + d(xl, bd))
        return jnp.concatenate(outs, axis=-1)

    w_raw = -_softplus(-(w0 + _dot(jnp.tanh(lo), wup_ref[...]))) - 0.5
    lw_ref[...] = -jnp.exp(w_raw)
    a = _sigmoid(a0 + _dot(lo, aup_ref[...]))
    if first_layer:
        vout_ref[...] = v
    else:
        v_mix = _sigmoid(v0 + _dot(lo, vup_ref[...]))
        v = v + (vfirst_ref[...] - v) * v_mix
    kk = k * k_k
    kk = kk / jnp.maximum(jnp.sqrt(head_sum(kk * kk)), L2_EPS)
    k = k * (1.0 + (a - 1.0) * k_a)
    bonus = head_sum(r * k * r_k) * v

    rr_ref[...] = r
    kn_ref[...] = kk
    kk_ref[...] = k
    be_ref[...] = kk * a
    vv_ref[...] = v

    code = code_ref[...]
    tri = tri_ref[...]
    lane = lax.broadcasted_iota(jnp.int32, (1, LANES), 1)
    m_a = lane < RW_HEAD
    row_i = lax.broadcasted_iota(jnp.int32, (LANES, LANES), 0)
    col_i = lax.broadcasted_iota(jnp.int32, (LANES, LANES), 1)
    same_head = (row_i < RW_HEAD) == (col_i < RW_HEAD)
    strict = code >= 0
    incl = code >= -1
    inv_dot = _dot

    def chunk_body(ci, carry):
        rows = pl.ds(pl.multiple_of(ci * c, c), c)
        lw = lw_ref[rows, :]
        cum = _sel_dot(tri, lw, 3)
        cum_prev = cum - lw
        c_mid = cum[c // 2 - 1:c // 2, :]
        c_end = cum[c - 1:c, :]
        rc, kc, knc, bc, vc = rr_ref[rows, :], kk_ref[rows, :], kn_ref[rows, :], be_ref[rows, :], vv_ref[rows, :]
        e_cum = jnp.exp(cum)
        e_prev = jnp.exp(cum_prev)
        e_mid_fwd = jnp.exp(cum - c_mid)
        e_mid_prev = jnp.exp(cum_prev - c_mid)
        e_mid_bwd = jnp.exp(c_mid - cum)
        e_end = jnp.exp(c_end - cum)
        r_abs, kn_abs = rc * e_cum, knc * e_prev
        r_mid, kn_mid = rc * e_mid_fwd, knc * e_mid_prev
        k_mid, b_mid = kc * e_mid_bwd, bc * e_mid_bwd
        k_end, b_end = kc * e_end, bc * e_end
        w_end = jnp.exp(c_end)
        sls = [slice(p * LANES, (p + 1) * LANES) for p in range(n_pairs)]
        gs = []
        for sl in sls:
            knm, rm = kn_mid[:, sl], r_mid[:, sl]
            lhs = jnp.concatenate([jnp.where(m_a, knm, 0.0), jnp.where(m_a, 0.0, knm),
                                   jnp.where(m_a, rm, 0.0), jnp.where(m_a, 0.0, rm)], axis=0)
            rhs = jnp.concatenate([k_mid[:, sl], b_mid[:, sl]], axis=0)
            gs.append(_dot(lhs, rhs, NT))
        heads = [(p, hh) for p in range(n_pairs) for hh in range(2)]
        lows = [jnp.where(strict, gs[p][hh * c:(hh + 1) * c, c:2 * c], 0.0) for p, hh in heads]
        sts = [st_ref[p] for p in range(n_pairs)]
        a1v = [_dot(jnp.where(strict, gs[p][hh * c:(hh + 1) * c, 0:c], 0.0), vc[:, sls[p]]) for p, hh in heads]
        x1 = [_dot(kn_abs[:, sls[p]], sts[p], NT) + jnp.where(m_a, a1v[2 * p], a1v[2 * p + 1])
              for p in range(n_pairs)]
        ts = [jnp.where(code == -1, 1.0, 0.0) - jnp.where(code == 0, lo_h, 0.0) for lo_h in lows]
        for l in range(1, LEVELS):
            us = [inv_dot(t, jnp.where(code == l, lo_h, 0.0)) for t, lo_h in zip(ts, lows)]
            ts = [t - inv_dot(u, t) for t, u in zip(ts, us)]
        sa = [-jnp.where(m_a, _dot(ts[2 * p], x1[p]), _dot(ts[2 * p + 1], x1[p])) for p in range(n_pairs)]
        vs = [jnp.concatenate([vc[:, sls[p]], sa[p]], axis=0) for p in range(n_pairs)]
        a34 = [jnp.concatenate([jnp.where(incl, gs[p][(2 + hh) * c:(3 + hh) * c, 0:c], 0.0),
                                jnp.where(incl, gs[p][(2 + hh) * c:(3 + hh) * c, c:2 * c], 0.0)], axis=1)
               for p, hh in heads]
        ys = [_dot(a34[i], vs[i // 2]) for i in range(len(heads))]
        for p in range(n_pairs):
            sl = sls[p]
            zed = jnp.concatenate([k_end[:, sl], b_end[:, sl]], axis=0)
            st_ref[p] = sts[p] * w_end[:, sl] + jnp.where(same_head, _dot(vs[p], zed, TN), 0.0)
            y_ref[rows, sl] = _dot(r_abs[:, sl], sts[p], NT) + jnp.where(m_a, ys[2 * p], ys[2 * p + 1])
        return carry

    lax.fori_loop(0, n_chunks, chunk_body, 0)

    y = y_ref[...]
    inv_n = 1.0 / RW_HEAD
    mean = head_sum(y) * inv_n
    yc = y - mean
    var = head_sum(yc * yc) * inv_n
    yn = yc * lax.rsqrt(var + GN_EPS) * gn_g + gn_b
    o_ref[...] = (yn + bonus) * _silu(z)


def _rwkv_call(proj, mu4, mu_lo, par, wup, aup, vup, bd, tri, code, v_first, batch, seq, ct, col0):
    w = par.shape[-1]
    wl = mu_lo.shape[-1]
    nt = seq // ct
    first_layer = v_first is None
    col = lambda j: (lambda b, t: (b * nt + t, j))
    const = lambda b, t: (0, 0)
    row_spec = pl.BlockSpec((ct, w), lambda b, t: (b * nt + t, 0))
    in_specs = [
        pl.BlockSpec((ct, w), col(col0)),
        pl.BlockSpec((ct, w), col(col0 + 1)),
        pl.BlockSpec((ct, w), col(col0 + 2)),
        pl.BlockSpec((ct, w), col(col0 + 3)),
        pl.BlockSpec((ct, wl), col((col0 + 4) * (w // wl))),
        pl.BlockSpec(mu4.shape, const),
        pl.BlockSpec(mu_lo.shape, const),
        pl.BlockSpec(par.shape, const),
        pl.BlockSpec(wup.shape, const),
        pl.BlockSpec(aup.shape, const),
        pl.BlockSpec(vup.shape, const),
        pl.BlockSpec(bd.shape, const),
        pl.BlockSpec(tri.shape, const),
        pl.BlockSpec(code.shape, const),
    ]
    args = [proj, proj, proj, proj, proj, mu4, mu_lo, par, wup, aup, vup, bd, tri, code]
    out_sds = jax.ShapeDtypeStruct((batch * seq, w), F32)
    if first_layer:
        out_specs = [row_spec, row_spec]
        out_shape = [out_sds, out_sds]
    else:
        in_specs.append(row_spec)
        args.append(v_first)
        out_specs = row_spec
        out_shape = out_sds
    kern = functools.partial(_rwkv_kernel, n_chunks=ct // CHUNK, first_layer=first_layer)
    return pl.pallas_call(
        kern,
        grid=(batch, nt),
        in_specs=in_specs,
        out_specs=out_specs,
        out_shape=out_shape,
        scratch_shapes=[
            pltpu.VMEM((8, w), F32),
            pltpu.VMEM((8, wl), F32),
            pltpu.VMEM((w // LANES, LANES, LANES), F32),
            pltpu.VMEM((ct, w), F32),
            pltpu.VMEM((ct, w), F32),
            pltpu.VMEM((ct, w), F32),
            pltpu.VMEM((ct, w), F32),
            pltpu.VMEM((ct, w), F32),
            pltpu.VMEM((ct, w), F32),
            pltpu.VMEM((ct, w), F32),
        ],
        compiler_params=pltpu.CompilerParams(
            dimension_semantics=("parallel", "arbitrary"), vmem_limit_bytes=VMEM_LIMIT),
        name="rwkv7_branch_first" if first_layer else "rwkv7_branch",
    )(*args)


def _out_kernel(x_ref, ohg_ref, orw_ref, ghg_ref, grw_ref, gate_ref, wa_ref, wb_ref, wo_ref,
                lng_ref, lnb_ref, o_ref, *, alpha):
    d = functools.partial(lax.dot_general, dimension_numbers=NN, preferred_element_type=F32)
    y_hg = d(ohg_ref[...].astype(BF16), wa_ref[...])
    y_rw = d(orw_ref[...].astype(BF16), wb_ref[...])
    merged = _sigmoid(ghg_ref[...]) * y_hg + _sigmoid(grw_ref[...]) * y_rw
    out = d(merged.astype(BF16), wo_ref[...])
    xn = alpha * x_ref[...] + (1.0 + gate_ref[0]) * out
    mu = jnp.mean(xn, axis=-1, keepdims=True)
    xc = xn - mu
    var = jnp.mean(xc * xc, axis=-1, keepdims=True)
    o_ref[...] = xc * lax.rsqrt(var + LN_EPS) * lng_ref[...] + lnb_ref[...]


def _out_call(x2, o_hg, o_rw, proj, gate, wa, wb, wo, ln_g, ln_b, seq, tm, gate_col, alpha):
    bt, d = x2.shape
    wh = o_hg.shape[1]
    per_b = seq // tm
    row = lambda i: (i, 0)
    const = lambda i: (0, 0)
    return pl.pallas_call(
        functools.partial(_out_kernel, alpha=alpha),
        grid=(bt // tm,),
        in_specs=[
            pl.BlockSpec((tm, d), row),
            pl.BlockSpec((tm, wh), row),
            pl.BlockSpec((tm, wh), row),
            pl.BlockSpec((tm, d), lambda i: (i, gate_col)),
            pl.BlockSpec((tm, d), lambda i: (i, gate_col + 1)),
            pl.BlockSpec((1, 1, d), lambda i: (i // per_b, 0, 0)),
            pl.BlockSpec(wa.shape, const),
            pl.BlockSpec(wb.shape, const),
            pl.BlockSpec(wo.shape, const),
            pl.BlockSpec((1, d), const),
            pl.BlockSpec((1, d), const),
        ],
        out_specs=pl.BlockSpec((tm, d), row),
        out_shape=jax.ShapeDtypeStruct((bt, d), F32),
        compiler_params=pltpu.CompilerParams(
            dimension_semantics=("parallel",), vmem_limit_bytes=VMEM_LIMIT),
        name="merge_out_proj_norm",
    )(x2, o_hg, o_rw, proj, proj, gate, wa, wb, wo, ln_g, ln_b)


def kernel(x, c, w_ada, b_ada, w_in, hg_lower_bounds, hg_norm_g, rw_mu, rw_w0, rw_w_up, rw_a0, rw_a_up,
           rw_k_k, rw_k_a, rw_r_k, rw_v0, rw_v_down, rw_v_up, rw_gn_g, rw_gn_b, w_branch_hg, w_branch_rw,
           w_out, ln_g, ln_b):
    batch, seq, d = x.shape
    depth = w_in.shape[0]
    hgw = hg_norm_g.shape[1]
    rww = rw_w0.shape[1]
    r_decay = rw_w_up.shape[1]
    r_icl = rw_a_up.shape[1]
    r_vres = rw_v_up.shape[1]
    hg_cols = 4 * hgw
    rw_main = 4 * rww
    lo_w = 2 * LANES
    assert r_decay + r_icl + r_vres <= lo_w and hgw == rww and d == 2 * hgw
    assert seq % CHUNK == 0
    alpha = float((2 * depth) ** 0.25)

    lb_soft = jax.nn.softmax(hg_lower_bounds.astype(F32), axis=0)
    lower_bounds = jnp.cumsum(lb_soft, axis=0) - lb_soft[0]

    seg = jnp.asarray(_hgrn_segment_matrix(CHUNK), BF16)
    code = jnp.asarray(_split_code(CHUNK), jnp.int32)
    tri = jnp.asarray(np.tril(np.ones((CHUNK, CHUNK), np.float32)), BF16)
    bd = jnp.asarray(_block_ones(2 * LANES, RW_HEAD), BF16)

    c_pad = jnp.zeros((8, d), F32).at[:batch].set(c)
    cond = _ada_call(c_pad, w_ada, b_ada)

    x2 = x.reshape(batch * seq, d)
    tm = min(1024, seq)
    ct = min(512, seq)
    v_first = None
    for l in range(depth):
        shift = cond[l, :batch, 0:d].reshape(batch, 1, d)
        scale = cond[l, :batch, d:2 * d].reshape(batch, 1, d)
        gate = cond[l, :batch, 2 * d:3 * d].reshape(batch, 1, d)

        wl = w_in[l]
        w_hg = wl[:, :hg_cols]
        w_rw = wl[:, hg_cols:hg_cols + rw_main]
        w_lo = wl[:, hg_cols + rw_main:hg_cols + rw_main + r_decay + r_icl]
        w_gate = wl[:, hg_cols + rw_main + r_decay + r_icl:]
        w_vd = rw_v_down[l - 1] if l > 0 else jnp.zeros((d, r_vres), F32)
        pad = jnp.zeros((d, lo_w - (r_decay + r_icl + r_vres)), F32)
        w_cat = jnp.concatenate([w_hg, w_gate, w_rw, w_lo, w_vd, pad], axis=1).astype(BF16)
        n_cat = w_cat.shape[1]
        tn = n_cat // 5 if (n_cat // 5) % LANES == 0 and n_cat % 5 == 0 else lo_w
        proj = _proj_call(x2, shift, scale, w_cat, seq, tm, tn)

        o_hg = _hgrn_call(proj, lower_bounds[l].reshape(1, hgw), hg_norm_g[l].reshape(1, hgw),
                          seg, code, batch, seq, ct)

        mu_l = rw_mu[l]
        mu4 = jnp.zeros((8, rww), F32).at[0:4].set(mu_l[:rw_main].reshape(4, rww))
        mu_lo = jnp.zeros((1, lo_w), F32).at[0, :r_decay + r_icl].set(mu_l[rw_main:])
        v0 = rw_v0[l - 1] if l > 0 else jnp.zeros((rww,), F32)
        par = jnp.stack([rw_w0[l], rw_a0[l], rw_k_k[l], rw_k_a[l], rw_r_k[l].reshape(rww),
                         rw_gn_g[l], rw_gn_b[l], v0], axis=0)
        wup = jnp.zeros((lo_w, rww), F32).at[0:r_decay].set(rw_w_up[l]).astype(BF16)
        aup = jnp.zeros((lo_w, rww), F32).at[r_decay:r_decay + r_icl].set(rw_a_up[l]).astype(BF16)
        vup_l = rw_v_up[l - 1] if l > 0 else jnp.zeros((r_vres, rww), F32)
        vup = jnp.zeros((lo_w, rww), F32).at[r_decay + r_icl:r_decay + r_icl + r_vres].set(vup_l).astype(BF16)
        col0 = (hg_cols + w_gate.shape[1]) // rww
        res = _rwkv_call(proj, mu4, mu_lo, par, wup, aup, vup, bd, tri, code, v_first, batch, seq, ct, col0)
        if l == 0:
            o_rw, v_first = res
        else:
            o_rw = res

        x2 = _out_call(x2, o_hg, o_rw, proj, gate, w_branch_hg[l].astype(BF16), w_branch_rw[l].astype(BF16),
                       w_out[l].astype(BF16), ln_g[l].reshape(1, d), ln_b[l].reshape(1, d), seq,
                       min(512, seq), hg_cols // d, alpha)
    return x2.reshape(batch, seq, d)
```

```python
import functools
import math

import numpy as np
import jax
import jax.numpy as jnp
from jax import lax
from jax.experimental import pallas as pl
from jax.experimental.pallas import tpu as pltpu

F32 = jnp.float32
BF16 = jnp.bfloat16

HG_DK = 128
RW_HEAD = 64
LN_EPS = 1e-5
RMS_EPS = 1e-6
GN_EPS = 64e-5
L2_EPS = 1e-12
LB_FLOOR = 1e-30

LANES = 128
SUBLANES = 8
CHUNK = 128
LEVELS = int(math.log2(CHUNK))
SMALL_LEVELS = int(math.log2(SUBLANES))
VMEM_LIMIT = 56 * 1024 * 1024

NT = (((1,), (1,)), ((), ()))
TN = (((0,), (0,)), ((), ()))
NN = (((1,), (0,)), ((), ()))


def _dot(a, b, dims=NN):
    return lax.dot_general(a.astype(BF16), b.astype(BF16), dims, preferred_element_type=F32)


def _split2(x):
    hi = x.astype(BF16)
    lo = (x - hi.astype(F32)).astype(BF16)
    return hi, lo


def _dot3(a, b):
    ah, al = _split2(a)
    bh, bl = _split2(b)
    d = functools.partial(lax.dot_general, dimension_numbers=NN, preferred_element_type=F32)
    return d(ah, bh) + (d(ah, bl) + d(al, bh))


def _sel_dot(m01, x):
    w = x.shape[-1]
    out = lax.dot_general(m01, jnp.concatenate(_split2(x), axis=-1), NN, preferred_element_type=F32)
    return out[:, 0:w] + out[:, w:2 * w]


def _sigmoid(x):
    return 1.0 / (1.0 + jnp.exp(-x))


def _silu(x):
    return x * _sigmoid(x)


def _sigmoid_pair(x):
    t = jnp.exp(-jnp.abs(x))
    big = 1.0 / (1.0 + t)
    small = t * big
    pos = x >= 0.0
    return jnp.where(pos, big, small), jnp.where(pos, small, big)


def _split_code(c):
    t = np.arange(c)[:, None]
    s = np.arange(c)[None, :]
    x = t ^ s
    code = np.where(x > 0, np.floor(np.log2(np.maximum(x, 1))).astype(np.int32), -1)
    code = np.where(s > t, -2, code)
    return code.astype(np.int32)


def _hgrn_segment_matrix(c):
    blocks = []
    j = np.arange(c)[None, :]
    t = np.arange(c)[:, None]
    for l in range(SMALL_LEVELS):
        m = 1 << l
        start = (t // (2 * m)) * (2 * m)
        mid = start + m - 1
        second = ((t // m) % 2) == 1
        blk = np.where(second, (j > mid) & (j <= t), (j > t) & (j <= mid))
        blocks.append(blk)
    blocks.append(j <= t)
    return np.concatenate(blocks, axis=0).astype(np.float32)


def _block_ones(n, blk):
    i = np.arange(n)
    return (i[:, None] // blk == i[None, :] // blk).astype(np.float32)


def _ada_kernel(c_ref, w_ref, b_ref, o_ref):
    sc = _silu(c_ref[...])
    o_ref[0] = _dot3(sc, w_ref[0]) + b_ref[0]


def _ada_call(c_pad, w_ada, b_ada):
    depth, d, d3 = w_ada.shape
    rows = c_pad.shape[0]
    tn = d
    return pl.pallas_call(
        _ada_kernel,
        grid=(depth, d3 // tn),
        in_specs=[
            pl.BlockSpec((rows, d), lambda l, j: (0, 0)),
            pl.BlockSpec((1, d, tn), lambda l, j: (l, 0, j)),
            pl.BlockSpec((1, 1, tn), lambda l, j: (l, 0, j)),
        ],
        out_specs=pl.BlockSpec((1, rows, tn), lambda l, j: (l, 0, j)),
        out_shape=jax.ShapeDtypeStruct((depth, rows, d3), F32),
        compiler_params=pltpu.CompilerParams(
            dimension_semantics=("parallel", "parallel"), vmem_limit_bytes=VMEM_LIMIT),
        name="adaln_cond",
    )(c_pad, w_ada, b_ada.reshape(depth, 1, d3))


def _proj_kernel(x_ref, shift_ref, scale_ref, w_ref, o_ref, h_ref):
    @pl.when(pl.program_id(1) == 0)
    def _():
        h = x_ref[...] * (1.0 + scale_ref[0]) + shift_ref[0]
        h_ref[...] = h.astype(BF16)

    o_ref[...] = lax.dot_general(h_ref[...], w_ref[...], NN, preferred_element_type=F32)


def _proj_call(x2, shift, scale, w_cat, seq, tm, tn):
    bt, d = x2.shape
    n = w_cat.shape[1]
    per_b = seq // tm
    return pl.pallas_call(
        _proj_kernel,
        grid=(bt // tm, n // tn),
        in_specs=[
            pl.BlockSpec((tm, d), lambda i, j: (i, 0)),
            pl.BlockSpec((1, 1, d), lambda i, j: (i // per_b, 0, 0)),
            pl.BlockSpec((1, 1, d), lambda i, j: (i // per_b, 0, 0)),
            pl.BlockSpec((d, tn), lambda i, j: (0, j)),
        ],
        out_specs=pl.BlockSpec((tm, tn), lambda i, j: (i, j)),
        out_shape=jax.ShapeDtypeStruct((bt, n), F32),
        scratch_shapes=[pltpu.VMEM((tm, d), BF16)],
        compiler_params=pltpu.CompilerParams(
            dimension_semantics=("parallel", "arbitrary"), vmem_limit_bytes=VMEM_LIMIT),
        name="modulate_in_proj",
    )(x2, shift, scale, w_cat)


def _hgrn_kernel(q_ref, f_ref, i_ref, z_ref, lb_ref, g_ref, seg_ref, code_ref, o_ref, st_ref, *, n_heads):
    c = CHUNK
    n_batch = q_ref.shape[0]

    @pl.when(pl.program_id(0) == 0)
    def _():
        st_ref[...] = jnp.zeros_like(st_ref)

    lb = lb_ref[...]
    lb_floor = jnp.maximum(lb, LB_FLOOR)
    code = code_ref[...]
    seg = seg_ref[...]
    g = g_ref[...]

    ks, d_alls = [], []
    for b in range(n_batch):
        sig, sig_neg = _sigmoid_pair(f_ref[b])
        log_f = jnp.log(lb_floor + (1.0 - lb) * sig)
        ks.append((1.0 - lb) * sig_neg)
        d_alls.append(_sel_dot(seg, -log_f))

    for b in range(n_batch):
        d_all = d_alls[b]
        nb = d_all[SMALL_LEVELS * c:(SMALL_LEVELS + 1) * c, :]
        e_b = jnp.exp(-nb)
        e_suf = jnp.exp(nb - nb[c - 1:c, :])
        e_last = e_b[c - 1:c, :]
        e_lvl = [jnp.exp(-d_all[l * c:(l + 1) * c, :]) for l in range(SMALL_LEVELS)]
        for l in range(SMALL_LEVELS, LEVELS):
            m = 1 << l
            ref = jnp.concatenate(
                [jnp.broadcast_to(nb[s + m - 1:s + m, :], (2 * m, nb.shape[1])) for s in range(0, c, 2 * m)], axis=0)
            e_lvl.append(jnp.exp(-jnp.abs(nb - ref)))
        q, k, v, z = q_ref[b], ks[b], i_ref[b], z_ref[b]
        for h in range(n_heads):
            sl = slice(h * HG_DK, (h + 1) * HG_DK)
            qh, kh, vh = q[:, sl], k[:, sl], v[:, sl]
            sc = jnp.where(code == -1, _dot(qh, kh, NT), 0.0)
            for l in range(LEVELS):
                e = e_lvl[l][:, sl]
                sc = jnp.where(code == l, _dot(qh * e, kh * e, NT), sc)
            st = st_ref[b, h]
            o = _dot(sc, vh) + _dot(qh * e_b[:, sl], st, NT)
            st_ref[b, h] = st * e_last[:, sl] + _dot(vh, kh * e_suf[:, sl], TN)
            o = o * lax.rsqrt(jnp.mean(o * o, axis=-1, keepdims=True) + RMS_EPS)
            o_ref[b, :, sl] = (o * g[:, sl] * _silu(z[:, sl])).astype(o_ref.dtype)


def _hgrn_call(proj3, lb, norm_g, seg, code):
    batch, seq, _ = proj3.shape
    w = lb.shape[-1]
    n_heads = w // HG_DK
    col = lambda j: (lambda t: (0, t, j))
    const = lambda t: (0, 0)
    blk = (batch, CHUNK, w)
    return pl.pallas_call(
        functools.partial(_hgrn_kernel, n_heads=n_heads),
        grid=(seq // CHUNK,),
        in_specs=[
            pl.BlockSpec(blk, col(0)),
            pl.BlockSpec(blk, col(1)),
            pl.BlockSpec(blk, col(2)),
            pl.BlockSpec(blk, col(3)),
            pl.BlockSpec((1, w), const),
            pl.BlockSpec((1, w), const),
            pl.BlockSpec(seg.shape, const),
            pl.BlockSpec(code.shape, const),
        ],
        out_specs=pl.BlockSpec(blk, col(0)),
        out_shape=jax.ShapeDtypeStruct((batch, seq, w), BF16),
        scratch_shapes=[pltpu.VMEM((batch, n_heads, HG_DK, HG_DK), F32)],
        compiler_params=pltpu.CompilerParams(
            dimension_semantics=("arbitrary",), vmem_limit_bytes=VMEM_LIMIT),
        name="hgrn2_branch",
    )(proj3, proj3, proj3, proj3, lb, norm_g, seg, code)


def _rwkv_kernel(*refs, first_layer):
    (r_ref, k_ref, v_ref, z_ref, lo_ref, mu_ref, mulo_ref, par_ref, wup_ref, aup_ref, vup_ref,
     bd_ref, tri_ref, code_ref) = refs[:14]
    if first_layer:
        vfirst_ref, (o_ref, vout_ref) = None, refs[14:16]
    else:
        vfirst_ref, o_ref, vout_ref = refs[14], refs[15], None
    prev_ref, prevlo_ref, st_ref = refs[16:]
    c = CHUNK
    n_batch, _, w = r_ref.shape
    n_pairs = w // LANES

    @pl.when(pl.program_id(0) == 0)
    def _():
        prev_ref[...] = jnp.zeros_like(prev_ref)
        prevlo_ref[...] = jnp.zeros_like(prevlo_ref)
        st_ref[...] = jnp.zeros_like(st_ref)

    row0 = lax.broadcasted_iota(jnp.int32, (c, 1), 0) == 0
    bd = bd_ref[...]
    code = code_ref[...]
    tri = tri_ref[...]
    lane = lax.broadcasted_iota(jnp.int32, (1, LANES), 1)
    m_a = lane < RW_HEAD
    row_i = lax.broadcasted_iota(jnp.int32, (LANES, LANES), 0)
    col_i = lax.broadcasted_iota(jnp.int32, (LANES, LANES), 1)
    same_head = (row_i < RW_HEAD) == (col_i < RW_HEAD)
    strict = code >= 0
    incl = code >= -1
    eye = jnp.where(code == -1, 1.0, 0.0)
    sls = [slice(p * LANES, (p + 1) * LANES) for p in range(n_pairs)]

    def head_sum(x):
        return _dot(x, bd)

    def shifted(p, prev_row, m):
        prev = jnp.where(row0, prev_row, pltpu.roll(p, 1, 0))
        return p + (prev - p) * m

    lo_cache = {}

    def low_rank_inputs(b):
        if b not in lo_cache:
            p_lo = lo_ref[b]
            lo = shifted(p_lo, prevlo_ref[b, 0:1, :], mulo_ref[...])
            prevlo_ref[b, 0:1, :] = p_lo[c - 1:c, :]
            lo_cache[b] = (jnp.tanh(lo).astype(BF16), lo.astype(BF16))
        return lo_cache[b]

    def prep(b, p, pre):
        sl = sls[p]
        w0, a0, k_k, k_a, r_k, _, _, v0 = (par_ref[i:i + 1, sl] for i in range(8))
        tanh_lo, lo = low_rank_inputs(b)
        p_r, p_k, p_v, p_z = r_ref[b, :, sl], k_ref[b, :, sl], v_ref[b, :, sl], z_ref[b, :, sl]
        r = shifted(p_r, prev_ref[b, 0:1, sl], mu_ref[0:1, sl])
        k = shifted(p_k, prev_ref[b, 1:2, sl], mu_ref[1:2, sl])
        v = shifted(p_v, prev_ref[b, 2:3, sl], mu_ref[2:3, sl])
        z = shifted(p_z, prev_ref[b, 3:4, sl], mu_ref[3:4, sl])
        prev_ref[b, 0:1, sl] = p_r[c - 1:c, :]
        prev_ref[b, 1:2, sl] = p_k[c - 1:c, :]
        prev_ref[b, 2:3, sl] = p_v[c - 1:c, :]
        prev_ref[b, 3:4, sl] = p_z[c - 1:c, :]
        gate = _silu(z)
        kk = k * k_k
        kk_sq = (kk * kk).astype(BF16)
        yield
        w_lin = lax.dot_general(tanh_lo, wup_ref[:, sl], NN, preferred_element_type=F32)
        a_lin = lax.dot_general(lo, aup_ref[:, sl], NN, preferred_element_type=F32)
        if not first_layer:
            v_lin = lax.dot_general(lo, vup_ref[:, sl], NN, preferred_element_type=F32)
        kk_ss = lax.dot_general(kk_sq, bd, NN, preferred_element_type=F32)
        yield
        lw = -math.exp(-0.5) * _sigmoid(w0 + w_lin)
        a = _sigmoid(a0 + a_lin)
        if first_layer:
            vout_ref[b, :, sl] = v
        else:
            v = v + (vfirst_ref[b, :, sl] - v) * _sigmoid(v0 + v_lin)
        kk = kk / jnp.maximum(jnp.sqrt(kk_ss), L2_EPS)
        k = k * (1.0 + (a - 1.0) * k_a)
        be = kk * a
        rkr = (r * k * r_k).astype(BF16)
        lw_parts = jnp.concatenate(_split2(lw), axis=-1)
        yield
        bonus_sum = lax.dot_general(rkr, bd, NN, preferred_element_type=F32)
        cum3 = lax.dot_general(tri, lw_parts, NN, preferred_element_type=F32)
        yield
        bonus = bonus_sum * v
        cum = cum3[:, 0:LANES] + cum3[:, LANES:2 * LANES]
        cum_prev = cum - lw
        c_mid = cum[c // 2 - 1:c // 2, :]
        c_end = cum[c - 1:c, :]
        e_mid_bwd = jnp.exp(c_mid - cum)
        e_end = jnp.exp(c_end - cum)
        pre[b, p] = dict(
            r_abs=(r * jnp.exp(cum)).astype(BF16), kn_abs=(kk * jnp.exp(cum_prev)).astype(BF16),
            r_mid=(r * jnp.exp(cum - c_mid)).astype(BF16), kn_mid=(kk * jnp.exp(cum_prev - c_mid)).astype(BF16),
            k_mid=(k * e_mid_bwd).astype(BF16), b_mid=(be * e_mid_bwd).astype(BF16),
            k_end=(k * e_end).astype(BF16), b_end=(be * e_end).astype(BF16),
            w_end=jnp.exp(c_end), v=v.astype(BF16), bonus=bonus, gate=gate)

    def recur(pairs, pre, ys):
        n = len(pairs)
        heads = [(i, hh) for i in range(n) for hh in range(2)]
        ch = [pre[bp] for bp in pairs]
        gs = []
        for d in ch:
            zero = jnp.zeros_like(d["kn_mid"])
            lhs = jnp.concatenate([jnp.where(m_a, d["kn_mid"], zero), jnp.where(m_a, zero, d["kn_mid"]),
                                   jnp.where(m_a, d["r_mid"], zero), jnp.where(m_a, zero, d["r_mid"])], axis=0)
            rhs = jnp.concatenate([d["k_mid"], d["b_mid"]], axis=0)
            gs.append(_dot(lhs, rhs, NT))
        yield
        sts = [st_ref[b, p] for b, p in pairs]
        lows = [jnp.where(strict, gs[i][hh * c:(hh + 1) * c, c:2 * c], 0.0) for i, hh in heads]
        a1v = [_dot(jnp.where(strict, gs[i][hh * c:(hh + 1) * c, 0:c], 0.0), ch[i]["v"]) for i, hh in heads]
        x1 = [_dot(ch[i]["kn_abs"], sts[i], NT) + jnp.where(m_a, a1v[2 * i], a1v[2 * i + 1]) for i in range(n)]
        yield
        ts = [eye - jnp.where(code == 0, lo_h, 0.0) for lo_h in lows]
        for l in range(1, LEVELS):
            m = 1 << l
            offs = [jnp.where(code == l, lo_h, 0.0) for lo_h in lows]
            if m < SUBLANES:
                us = [_dot(t, off) for t, off in zip(ts, offs)]
                yield
                ts = [t - _dot(u, t) for t, u in zip(ts, us)]
                yield
            else:
                lower = [jnp.concatenate([t[s + m:s + 2 * m, :] for s in range(0, c, 2 * m)], axis=0) for t in ts]
                us = [_dot(lo_rows, off) for lo_rows, off in zip(lower, offs)]
                yield
                new = [lo_rows - _dot(u, t) for lo_rows, u, t in zip(lower, us, ts)]
                ts = [jnp.concatenate(
                    [piece for j, s in enumerate(range(0, c, 2 * m))
                     for piece in (t[s:s + m, :], nw[j * m:(j + 1) * m, :])], axis=0) for t, nw in zip(ts, new)]
                yield
        sa = [(-jnp.where(m_a, _dot(ts[2 * i], x1[i]), _dot(ts[2 * i + 1], x1[i]))).astype(BF16) for i in range(n)]
        vs = [jnp.concatenate([ch[i]["v"], sa[i]], axis=0) for i in range(n)]
        yield
        a34 = [jnp.concatenate([jnp.where(incl, gs[i][(2 + hh) * c:(3 + hh) * c, 0:c], 0.0),
                                jnp.where(incl, gs[i][(2 + hh) * c:(3 + hh) * c, c:2 * c], 0.0)], axis=1)
               for i, hh in heads]
        yh = [_dot(a34[j], vs[j // 2]) for j in range(len(heads))]
        yield
        for i, (b, p) in enumerate(pairs):
            zed = jnp.concatenate([ch[i]["k_end"], ch[i]["b_end"]], axis=0)
            st_ref[b, p] = sts[i] * ch[i]["w_end"] + jnp.where(same_head, _dot(vs[i], zed, TN), 0.0)
            ys[b, p] = _dot(ch[i]["r_abs"], sts[i], NT) + jnp.where(m_a, yh[2 * i], yh[2 * i + 1])
        yield

    def finish(b, p, y, d):
        sl = sls[p]
        inv_n = 1.0 / RW_HEAD
        y_sum = head_sum(y)
        yield
        yc = y - y_sum * inv_n
        yc_sq = (yc * yc).astype(BF16)
        yield
        var = lax.dot_general(yc_sq, bd, NN, preferred_element_type=F32) * inv_n
        yield
        yn = yc * lax.rsqrt(var + GN_EPS) * par_ref[5:6, sl] + par_ref[6:7, sl]
        o_ref[b, :, sl] = ((yn + d["bonus"]) * d["gate"]).astype(o_ref.dtype)

    def run_all(gens):
        gens = list(gens)
        while gens:
            gens = [g for g in gens if next(g, StopIteration) is not StopIteration]

    group = 2 if n_batch % 2 == 0 else 1
    groups = [[(b, p) for b in range(g0, g0 + group) for p in range(n_pairs)] for g0 in range(0, n_batch, group)]
    pre, ys = {}, {}
    run_all(prep(*bp, pre) for bp in groups[0])
    for gi, pairs in enumerate(groups):
        fill = [prep(*bp, pre) for bp in (groups[gi + 1] if gi + 1 < len(groups) else [])]
        fill += [finish(*bp, ys[bp], pre[bp]) for bp in (groups[gi - 1] if gi > 0 else [])]
        n_stages = 2 * LEVELS + 2
        start = [(j * (n_stages - 4)) // max(len(fill), 1) for j in range(len(fill))]
        active = []
        for s, _ in enumerate(recur(pairs, pre, ys)):
            active += [g for j, g in enumerate(fill) if start[j] == s]
            active = [g for g in active if next(g, StopIteration) is not StopIteration]
        run_all(active)
    run_all(finish(*bp, ys[bp], pre[bp]) for bp in groups[-1])


def _rwkv_call(proj3, mu4, mu_lo, par, wup, aup, vup, bd, tri, code, v_first, col0):
    batch, seq, _ = proj3.shape
    w = par.shape[-1]
    wl = mu_lo.shape[-1]
    first_layer = v_first is None
    col = lambda j: (lambda t: (0, t, j))
    const = lambda t: (0, 0)
    blk = (batch, CHUNK, w)
    row_spec = pl.BlockSpec(blk, col(0))
    in_specs = [
        pl.BlockSpec(blk, col(col0)),
        pl.BlockSpec(blk, col(col0 + 1)),
        pl.BlockSpec(blk, col(col0 + 2)),
        pl.BlockSpec(blk, col(col0 + 3)),
        pl.BlockSpec((batch, CHUNK, wl), col((col0 + 4) * (w // wl))),
        pl.BlockSpec(mu4.shape, const),
        pl.BlockSpec(mu_lo.shape, const),
        pl.BlockSpec(par.shape, const),
        pl.BlockSpec(wup.shape, const),
        pl.BlockSpec(aup.shape, const),
        pl.BlockSpec(vup.shape, const),
        pl.BlockSpec(bd.shape, const),
        pl.BlockSpec(tri.shape, const),
        pl.BlockSpec(code.shape, const),
    ]
    args = [proj3, proj3, proj3, proj3, proj3, mu4, mu_lo, par, wup, aup, vup, bd, tri, code]
    out_sds = jax.ShapeDtypeStruct((batch, seq, w), BF16)
    if first_layer:
        out_specs = [row_spec, row_spec]
        out_shape = [out_sds, jax.ShapeDtypeStruct((batch, seq, w), F32)]
    else:
        in_specs.append(row_spec)
        args.append(v_first)
        out_specs = row_spec
        out_shape = out_sds
    return pl.pallas_call(
        functools.partial(_rwkv_kernel, first_layer=first_layer),
        grid=(seq // CHUNK,),
        in_specs=in_specs,
        out_specs=out_specs,
        out_shape=out_shape,
        scratch_shapes=[
            pltpu.VMEM((batch, SUBLANES, w), F32),
            pltpu.VMEM((batch, SUBLANES, wl), F32),
            pltpu.VMEM((batch, w // LANES, LANES, LANES), F32),
        ],
        compiler_params=pltpu.CompilerParams(
            dimension_semantics=("arbitrary",), vmem_limit_bytes=VMEM_LIMIT),
        name="rwkv7_branch_first" if first_layer else "rwkv7_branch",
    )(*args)


def _out_kernel(x_ref, ohg_ref, orw_ref, ghg_ref, grw_ref, gate_ref, wa_ref, wb_ref, wo_ref,
                lng_ref, lnb_ref, o_ref, *, alpha):
    d = functools.partial(lax.dot_general, dimension_numbers=NN, preferred_element_type=F32)
    y_hg = d(ohg_ref[...].astype(BF16), wa_ref[...])
    y_rw = d(orw_ref[...].astype(BF16), wb_ref[...])
    merged = _sigmoid(ghg_ref[...]) * y_hg + _sigmoid(grw_ref[...]) * y_rw
    out = d(merged.astype(BF16), wo_ref[...])
    xn = alpha * x_ref[...] + (1.0 + gate_ref[0]) * out
    mu = jnp.mean(xn, axis=-1, keepdims=True)
    xc = xn - mu
    var = jnp.mean(xc * xc, axis=-1, keepdims=True)
    o_ref[...] = xc * lax.rsqrt(var + LN_EPS) * lng_ref[...] + lnb_ref[...]


def _out_call(x2, o_hg, o_rw, proj, gate, wa, wb, wo, ln_g, ln_b, seq, tm, gate_col, alpha):
    bt, d = x2.shape
    wh = o_hg.shape[1]
    per_b = seq // tm
    row = lambda i: (i, 0)
    const = lambda i: (0, 0)
    return pl.pallas_call(
        functools.partial(_out_kernel, alpha=alpha),
        grid=(bt // tm,),
        in_specs=[
            pl.BlockSpec((tm, d), row),
            pl.BlockSpec((tm, wh), row),
            pl.BlockSpec((tm, wh), row),
            pl.BlockSpec((tm, d), lambda i: (i, gate_col)),
            pl.BlockSpec((tm, d), lambda i: (i, gate_col + 1)),
            pl.BlockSpec((1, 1, d), lambda i: (i // per_b, 0, 0)),
            pl.BlockSpec(wa.shape, const),
            pl.BlockSpec(wb.shape, const),
            pl.BlockSpec(wo.shape, const),
            pl.BlockSpec((1, d), const),
            pl.BlockSpec((1, d), const),
        ],
        out_specs=pl.BlockSpec((tm, d), row),
        out_shape=jax.ShapeDtypeStruct((bt, d), F32),
        compiler_params=pltpu.CompilerParams(
            dimension_semantics=("parallel",), vmem_limit_bytes=VMEM_LIMIT),
        name="merge_out_proj_norm",
    )(x2, o_hg, o_rw, proj, proj, gate, wa, wb, wo, ln_g, ln_b)


def kernel(x, c, w_ada, b_ada, w_in, hg_lower_bounds, hg_norm_g, rw_mu, rw_w0, rw_w_up, rw_a0, rw_a_up,
           rw_k_k, rw_k_a, rw_r_k, rw_v0, rw_v_down, rw_v_up, rw_gn_g, rw_gn_b, w_branch_hg, w_branch_rw,
           w_out, ln_g, ln_b):
    batch, seq, d = x.shape
    depth = w_in.shape[0]
    hgw = hg_norm_g.shape[1]
    rww = rw_w0.shape[1]
    r_decay = rw_w_up.shape[1]
    r_icl = rw_a_up.shape[1]
    r_vres = rw_v_up.shape[1]
    hg_cols = 4 * hgw
    rw_main = 4 * rww
    lo_w = 2 * LANES
    assert r_decay + r_icl + r_vres <= lo_w and hgw == rww and d == 2 * hgw
    assert seq % CHUNK == 0
    alpha = float((2 * depth) ** 0.25)

    lb_soft = jax.nn.softmax(hg_lower_bounds.astype(F32), axis=0)
    lower_bounds = jnp.cumsum(lb_soft, axis=0) - lb_soft[0]

    seg = jnp.asarray(_hgrn_segment_matrix(CHUNK), BF16)
    code = jnp.asarray(_split_code(CHUNK), jnp.int32)
    tri = jnp.asarray(np.tril(np.ones((CHUNK, CHUNK), np.float32)), BF16)
    bd = jnp.asarray(_block_ones(LANES, RW_HEAD), BF16)

    c_pad = jnp.zeros((8, d), F32).at[:batch].set(c)
    cond = _ada_call(c_pad, w_ada, b_ada)

    x2 = x.reshape(batch * seq, d)
    tm = min(1024, seq)
    v_first = None
    for l in range(depth):
        shift = cond[l, :batch, 0:d].reshape(batch, 1, d)
        scale = cond[l, :batch, d:2 * d].reshape(batch, 1, d)
        gate = cond[l, :batch, 2 * d:3 * d].reshape(batch, 1, d)

        wl = w_in[l]
        w_hg = wl[:, :hg_cols]
        w_rw = wl[:, hg_cols:hg_cols + rw_main]
        w_lo = wl[:, hg_cols + rw_main:hg_cols + rw_main + r_decay + r_icl]
        w_gate = wl[:, hg_cols + rw_main + r_decay + r_icl:]
        w_vd = rw_v_down[l - 1] if l > 0 else jnp.zeros((d, r_vres), F32)
        pad = jnp.zeros((d, lo_w - (r_decay + r_icl + r_vres)), F32)
        w_cat = jnp.concatenate([w_hg, w_gate, w_rw, w_lo, w_vd, pad], axis=1).astype(BF16)
        n_cat = w_cat.shape[1]
        tn = n_cat // 5 if (n_cat // 5) % LANES == 0 and n_cat % 5 == 0 else lo_w
        proj = _proj_call(x2, shift, scale, w_cat, seq, tm, tn)

        proj3 = proj.reshape(batch, seq, n_cat)
        o_hg = _hgrn_call(proj3, lower_bounds[l].reshape(1, hgw), hg_norm_g[l].reshape(1, hgw), seg, code)

        mu_l = rw_mu[l]
        mu4 = jnp.zeros((8, rww), F32).at[0:4].set(mu_l[:rw_main].reshape(4, rww))
        mu_lo = jnp.zeros((1, lo_w), F32).at[0, :r_decay + r_icl].set(mu_l[rw_main:])
        v0 = rw_v0[l - 1] if l > 0 else jnp.zeros((rww,), F32)
        par = jnp.stack([rw_w0[l], rw_a0[l], rw_k_k[l], rw_k_a[l], rw_r_k[l].reshape(rww),
                         rw_gn_g[l], rw_gn_b[l], v0], axis=0)
        wup = jnp.zeros((lo_w, rww), F32).at[0:r_decay].set(rw_w_up[l]).astype(BF16)
        aup = jnp.zeros((lo_w, rww), F32).at[r_decay:r_decay + r_icl].set(rw_a_up[l]).astype(BF16)
        vup_l = rw_v_up[l - 1] if l > 0 else jnp.zeros((r_vres, rww), F32)
        vup = jnp.zeros((lo_w, rww), F32).at[r_decay + r_icl:r_decay + r_icl + r_vres].set(vup_l).astype(BF16)
        col0 = (hg_cols + w_gate.shape[1]) // rww
        res = _rwkv_call(proj3, mu4, mu_lo, par, wup, aup, vup, bd, tri, code, v_first, col0)
        if l == 0:
            o_rw, v_first = res
        else:
            o_rw = res

        x2 = _out_call(x2, o_hg.reshape(batch * seq, hgw), o_rw.reshape(batch * seq, rww), proj, gate,
                       w_branch_hg[l].astype(BF16), w_branch_rw[l].astype(BF16), w_out[l].astype(BF16),
                       ln_g[l].reshape(1, d), ln_b[l].reshape(1, d), seq, min(512, seq), hg_cols // d, alpha)
    return x2.reshape(batch, seq, d)
```

```python
import functools
import math

import numpy as np
import jax
import jax.numpy as jnp
from jax import lax
from jax.experimental import pallas as pl
from jax.experimental.pallas import tpu as pltpu

F32 = jnp.float32
BF16 = jnp.bfloat16

HG_DK = 128
RW_HEAD = 64
LN_EPS = 1e-5
RMS_EPS = 1e-6
GN_EPS = 64e-5
L2_EPS = 1e-12
LB_FLOOR = 1e-30

LANES = 128
SUBLANES = 8
CHUNK = 128
LEVELS = int(math.log2(CHUNK))
SMALL_LEVELS = int(math.log2(SUBLANES))
VMEM_LIMIT = 56 * 1024 * 1024

NT = (((1,), (1,)), ((), ()))
TN = (((0,), (0,)), ((), ()))
NN = (((1,), (0,)), ((), ()))


def _dot(a, b, dims=NN):
    return lax.dot_general(a.astype(BF16), b.astype(BF16), dims, preferred_element_type=F32)


def _split2(x):
    hi = x.astype(BF16)
    lo = (x - hi.astype(F32)).astype(BF16)
    return hi, lo


def _dot3(a, b):
    ah, al = _split2(a)
    bh, bl = _split2(b)
    d = functools.partial(lax.dot_general, dimension_numbers=NN, preferred_element_type=F32)
    return d(ah, bh) + (d(ah, bl) + d(al, bh))


def _sel_dot(m01, x):
    w = x.shape[-1]
    out = lax.dot_general(m01, jnp.concatenate(_split2(x), axis=-1), NN, preferred_element_type=F32)
    return out[:, 0:w] + out[:, w:2 * w]


def _sigmoid(x):
    return 1.0 / (1.0 + jnp.exp(-x))


def _silu(x):
    return x * _sigmoid(x)


def _exp_neg(x):
    return jnp.exp2(x * (-math.log2(math.e)))


def _sigmoid_pair(x):
    t = jnp.exp(-jnp.abs(x))
    big = 1.0 / (1.0 + t)
    small = t * big
    pos = x >= 0.0
    return jnp.where(pos, big, small), jnp.where(pos, small, big)


def _split_code(c):
    t = np.arange(c)[:, None]
    s = np.arange(c)[None, :]
    x = t ^ s
    code = np.where(x > 0, np.floor(np.log2(np.maximum(x, 1))).astype(np.int32), -1)
    code = np.where(s > t, -2, code)
    return code.astype(np.int32)


def _hgrn_segment_matrix(c):
    blocks = []
    j = np.arange(c)[None, :]
    t = np.arange(c)[:, None]
    for l in range(SMALL_LEVELS):
        m = 1 << l
        start = (t // (2 * m)) * (2 * m)
        mid = start + m - 1
        second = ((t // m) % 2) == 1
        blk = np.where(second, (j > mid) & (j <= t), (j > t) & (j <= mid))
        blocks.append(blk)
    blocks.append(j <= t)
    return np.concatenate(blocks, axis=0).astype(np.float32)


def _block_ones(n, blk):
    i = np.arange(n)
    return (i[:, None] // blk == i[None, :] // blk).astype(np.float32)


def _ada_kernel(c_ref, w_ref, b_ref, o_ref):
    sc = _silu(c_ref[...])
    o_ref[0] = _dot3(sc, w_ref[0]) + b_ref[0]


def _ada_call(c_pad, w_ada, b_ada):
    depth, d, d3 = w_ada.shape
    rows = c_pad.shape[0]
    tn = d
    return pl.pallas_call(
        _ada_kernel,
        grid=(depth, d3 // tn),
        in_specs=[
            pl.BlockSpec((rows, d), lambda l, j: (0, 0)),
            pl.BlockSpec((1, d, tn), lambda l, j: (l, 0, j)),
            pl.BlockSpec((1, 1, tn), lambda l, j: (l, 0, j)),
        ],
        out_specs=pl.BlockSpec((1, rows, tn), lambda l, j: (l, 0, j)),
        out_shape=jax.ShapeDtypeStruct((depth, rows, d3), F32),
        compiler_params=pltpu.CompilerParams(
            dimension_semantics=("parallel", "parallel"), vmem_limit_bytes=VMEM_LIMIT),
        name="adaln_cond",
    )(c_pad, w_ada, b_ada.reshape(depth, 1, d3))


def _proj_kernel(x_ref, shift_ref, scale_ref, w_ref, of_ref, ob_ref, h_ref):
    j = pl.program_id(1)

    @pl.when(j == 0)
    def _():
        h = x_ref[...] * (1.0 + scale_ref[0]) + shift_ref[0]
        h_ref[...] = h.astype(BF16)
        of_ref[...] = lax.dot_general(h_ref[...], w_ref[...], NN, preferred_element_type=F32)

    @pl.when(j > 0)
    def _():
        ob_ref[...] = lax.dot_general(h_ref[...], w_ref[...], NN, preferred_element_type=F32).astype(BF16)


def _proj_call(x2, shift, scale, w_cat, seq, tm, tn):
    bt, d = x2.shape
    n = w_cat.shape[1]
    per_b = seq // tm
    return pl.pallas_call(
        _proj_kernel,
        grid=(bt // tm, n // tn),
        in_specs=[
            pl.BlockSpec((tm, d), lambda i, j: (i, 0)),
            pl.BlockSpec((1, 1, d), lambda i, j: (i // per_b, 0, 0)),
            pl.BlockSpec((1, 1, d), lambda i, j: (i // per_b, 0, 0)),
            pl.BlockSpec((d, tn), lambda i, j: (0, j)),
        ],
        out_specs=[
            pl.BlockSpec((tm, tn), lambda i, j: (i, 0)),
            pl.BlockSpec((tm, tn), lambda i, j: (i, jnp.maximum(j - 1, 0))),
        ],
        out_shape=[jax.ShapeDtypeStruct((bt, tn), F32), jax.ShapeDtypeStruct((bt, n - tn), BF16)],
        scratch_shapes=[pltpu.VMEM((tm, d), BF16)],
        compiler_params=pltpu.CompilerParams(
            dimension_semantics=("parallel", "arbitrary"), vmem_limit_bytes=VMEM_LIMIT),
        name="modulate_in_proj",
    )(x2, shift, scale, w_cat)


def _hgrn_kernel(q_ref, f_ref, i_ref, z_ref, lb_ref, g_ref, seg_ref, code_ref, o_ref, st_ref, *, n_heads):
    c = CHUNK
    n_batch = q_ref.shape[0]

    @pl.when(pl.program_id(0) == 0)
    def _():
        st_ref[...] = jnp.zeros_like(st_ref)

    lb = lb_ref[...]
    lb_floor = jnp.maximum(lb, LB_FLOOR)
    code = code_ref[...]
    seg = seg_ref[...]
    g = g_ref[...]

    ks, d_alls = [], []
    for b in range(n_batch):
        sig, sig_neg = _sigmoid_pair(f_ref[b])
        log_f = jnp.log(lb_floor + (1.0 - lb) * sig)
        ks.append((1.0 - lb) * sig_neg)
        d_alls.append(_sel_dot(seg, -log_f))

    sls = [slice(h * HG_DK, (h + 1) * HG_DK) for h in range(n_heads)]

    def scores(b):
        d_all = d_alls[b]
        nb = d_all[SMALL_LEVELS * c:(SMALL_LEVELS + 1) * c, :]
        e_last = _exp_neg(nb[c - 1:c, :])
        e_b = _exp_neg(nb).astype(BF16)
        e_suf = _exp_neg(nb[c - 1:c, :] - nb).astype(BF16)
        e_lvl = [_exp_neg(d_all[l * c:(l + 1) * c, :]).astype(BF16) for l in range(SMALL_LEVELS)]
        for l in range(SMALL_LEVELS, LEVELS):
            m = 1 << l
            ref = jnp.concatenate(
                [jnp.broadcast_to(nb[s + m - 1:s + m, :], (2 * m, nb.shape[1])) for s in range(0, c, 2 * m)], axis=0)
            e_lvl.append(_exp_neg(jnp.abs(nb - ref)).astype(BF16))
        q, k, v = q_ref[b].astype(BF16), ks[b].astype(BF16), i_ref[b].astype(BF16)
        out = []
        for sl in sls:
            qh, kh = q[:, sl], k[:, sl]
            sc = jnp.where(code == -1, _dot(qh, kh, NT), 0.0)
            for l in range(LEVELS):
                e = e_lvl[l][:, sl]
                sc = jnp.where(code == l, _dot(qh * e, kh * e, NT), sc)
            out.append(dict(sc=sc, v=v[:, sl], q_in=qh * e_b[:, sl], k_out=kh * e_suf[:, sl], e_last=e_last[:, sl]))
        return out

    def finish(b, items):
        for h, (sl, d) in enumerate(zip(sls, items)):
            st = st_ref[b, h]
            o = _dot(d["sc"], d["v"]) + _dot(d["q_in"], st, NT)
            st_ref[b, h] = st * d["e_last"] + _dot(d["v"], d["k_out"], TN)
            o = o * lax.rsqrt(jnp.mean(o * o, axis=-1, keepdims=True) + RMS_EPS)
            o_ref[b, :, sl] = (o * g[:, sl] * _silu(z_ref[b, :, sl].astype(F32))).astype(o_ref.dtype)

    pending = scores(0)
    for b in range(1, n_batch):
        nxt = scores(b)
        finish(b - 1, pending)
        pending = nxt
    finish(n_batch - 1, pending)


def _hgrn_call(pf3, pb3, lb, norm_g, seg, code):
    batch, seq, _ = pf3.shape
    w = lb.shape[-1]
    n_heads = w // HG_DK
    col = lambda j: (lambda t: (0, t, j))
    const = lambda t: (0, 0)
    blk = (batch, CHUNK, w)
    return pl.pallas_call(
        functools.partial(_hgrn_kernel, n_heads=n_heads),
        grid=(seq // CHUNK,),
        in_specs=[
            pl.BlockSpec(blk, col(0)),
            pl.BlockSpec(blk, col(1)),
            pl.BlockSpec(blk, col(0)),
            pl.BlockSpec(blk, col(1)),
            pl.BlockSpec((1, w), const),
            pl.BlockSpec((1, w), const),
            pl.BlockSpec(seg.shape, const),
            pl.BlockSpec(code.shape, const),
        ],
        out_specs=pl.BlockSpec(blk, col(0)),
        out_shape=jax.ShapeDtypeStruct((batch, seq, w), BF16),
        scratch_shapes=[pltpu.VMEM((batch, n_heads, HG_DK, HG_DK), F32)],
        compiler_params=pltpu.CompilerParams(
            dimension_semantics=("arbitrary",), vmem_limit_bytes=VMEM_LIMIT),
        name="hgrn2_branch",
    )(pf3, pf3, pb3, pb3, lb, norm_g, seg, code)


def _rwkv_kernel(*refs, first_layer):
    (r_ref, k_ref, v_ref, z_ref, lo_ref, mu_ref, mulo_ref, par_ref, wup_ref, aup_ref, vup_ref,
     bd_ref, tri_ref, code_ref) = refs[:14]
    if first_layer:
        vfirst_ref, (o_ref, vout_ref) = None, refs[14:16]
    else:
        vfirst_ref, o_ref, vout_ref = refs[14], refs[15], None
    prev_ref, prevlo_ref, st_ref = refs[16:]
    c = CHUNK
    n_batch, _, w = r_ref.shape
    n_pairs = w // LANES

    @pl.when(pl.program_id(0) == 0)
    def _():
        prev_ref[...] = jnp.zeros_like(prev_ref)
        prevlo_ref[...] = jnp.zeros_like(prevlo_ref)
        st_ref[...] = jnp.zeros_like(st_ref)

    row0 = lax.broadcasted_iota(jnp.int32, (c, 1), 0) == 0
    bd = bd_ref[...]
    code = code_ref[...]
    tri = tri_ref[...]
    lane = lax.broadcasted_iota(jnp.int32, (1, LANES), 1)
    m_a = lane < RW_HEAD
    row_i = lax.broadcasted_iota(jnp.int32, (LANES, LANES), 0)
    col_i = lax.broadcasted_iota(jnp.int32, (LANES, LANES), 1)
    same_head = (row_i < RW_HEAD) == (col_i < RW_HEAD)
    strict = code >= 0
    incl = code >= -1
    eye = jnp.where(code == -1, 1.0, 0.0)
    sls = [slice(p * LANES, (p + 1) * LANES) for p in range(n_pairs)]

    def head_sum(x):
        return _dot(x, bd)

    def shifted(p, prev_row, m):
        prev = jnp.where(row0, prev_row, pltpu.roll(p, 1, 0))
        return p + (prev - p) * m

    lo_cache = {}

    def low_rank_inputs(b):
        if b not in lo_cache:
            p_lo = lo_ref[b]
            lo = shifted(p_lo, prevlo_ref[b, 0:1, :], mulo_ref[...])
            prevlo_ref[b, 0:1, :] = p_lo[c - 1:c, :]
            lo_cache[b] = (jnp.tanh(lo).astype(BF16), lo.astype(BF16))
        return lo_cache[b]

    def prep(b, p, pre):
        sl = sls[p]
        w0, a0, k_k, k_a, r_k, _, _, v0 = (par_ref[i:i + 1, sl] for i in range(8))
        tanh_lo, lo = low_rank_inputs(b)
        p_r, p_k, p_v, p_z = (ref[b, :, sl].astype(F32) for ref in (r_ref, k_ref, v_ref, z_ref))
        r = shifted(p_r, prev_ref[b, 0:1, sl], mu_ref[0:1, sl])
        k = shifted(p_k, prev_ref[b, 1:2, sl], mu_ref[1:2, sl])
        v = shifted(p_v, prev_ref[b, 2:3, sl], mu_ref[2:3, sl])
        z = shifted(p_z, prev_ref[b, 3:4, sl], mu_ref[3:4, sl])
        prev_ref[b, 0:1, sl] = p_r[c - 1:c, :]
        prev_ref[b, 1:2, sl] = p_k[c - 1:c, :]
        prev_ref[b, 2:3, sl] = p_v[c - 1:c, :]
        prev_ref[b, 3:4, sl] = p_z[c - 1:c, :]
        gate = _silu(z)
        kk = k * k_k
        kk_sq = (kk * kk).astype(BF16)
        yield
        w_lin = lax.dot_general(tanh_lo, wup_ref[:, sl], NN, preferred_element_type=F32)
        a_lin = lax.dot_general(lo, aup_ref[:, sl], NN, preferred_element_type=F32)
        if not first_layer:
            v_lin = lax.dot_general(lo, vup_ref[:, sl], NN, preferred_element_type=F32)
        kk_ss = lax.dot_general(kk_sq, bd, NN, preferred_element_type=F32)
        yield
        lw = -math.exp(-0.5) * _sigmoid(w0 + w_lin)
        a = _sigmoid(a0 + a_lin)
        if first_layer:
            vout_ref[b, :, sl] = v
        else:
            v = v + (vfirst_ref[b, :, sl] - v) * _sigmoid(v0 + v_lin)
        kk = kk / jnp.maximum(jnp.sqrt(kk_ss), L2_EPS)
        k = k * (1.0 + (a - 1.0) * k_a)
        be = kk * a
        rkr = (r * k * r_k).astype(BF16)
        lw_parts = jnp.concatenate(_split2(lw), axis=-1)
        yield
        bonus_sum = lax.dot_general(rkr, bd, NN, preferred_element_type=F32)
        cum3 = lax.dot_general(tri, lw_parts, NN, preferred_element_type=F32)
        yield
        bonus = bonus_sum * v
        cum = cum3[:, 0:LANES] + cum3[:, LANES:2 * LANES]
        cum_prev = cum - lw
        c_mid = cum[c // 2 - 1:c // 2, :]
        c_end = cum[c - 1:c, :]
        e_mid_bwd = jnp.exp(c_mid - cum)
        e_end = jnp.exp(c_end - cum)
        pre[b, p] = dict(
            r_abs=(r * jnp.exp(cum)).astype(BF16), kn_abs=(kk * jnp.exp(cum_prev)).astype(BF16),
            r_mid=(r * jnp.exp(cum - c_mid)).astype(BF16), kn_mid=(kk * jnp.exp(cum_prev - c_mid)).astype(BF16),
            k_mid=(k * e_mid_bwd).astype(BF16), b_mid=(be * e_mid_bwd).astype(BF16),
            k_end=(k * e_end).astype(BF16), b_end=(be * e_end).astype(BF16),
            w_end=jnp.exp(c_end), v=v.astype(BF16), bonus=bonus, gate=gate)

    def recur(pairs, pre, ys):
        n = len(pairs)
        heads = [(i, hh) for i in range(n) for hh in range(2)]
        ch = [pre[bp] for bp in pairs]
        gs = []
        for d in ch:
            zero = jnp.zeros_like(d["kn_mid"])
            lhs = jnp.concatenate([jnp.where(m_a, d["kn_mid"], zero), jnp.where(m_a, zero, d["kn_mid"]),
                                   jnp.where(m_a, d["r_mid"], zero), jnp.where(m_a, zero, d["r_mid"])], axis=0)
            rhs = jnp.concatenate([d["k_mid"], d["b_mid"]], axis=0)
            gs.append(_dot(lhs, rhs, NT))
        yield
        sts = [st_ref[b, p] for b, p in pairs]
        lows = [jnp.where(strict, gs[i][hh * c:(hh + 1) * c, c:2 * c], 0.0) for i, hh in heads]
        a1v = [_dot(jnp.where(strict, gs[i][hh * c:(hh + 1) * c, 0:c], 0.0), ch[i]["v"]) for i, hh in heads]
        x1 = [_dot(ch[i]["kn_abs"], sts[i], NT) + jnp.where(m_a, a1v[2 * i], a1v[2 * i + 1]) for i in range(n)]
        yield
        ts = [eye - jnp.where(code == 0, lo_h, 0.0) for lo_h in lows]
        for l in range(1, LEVELS):
            m = 1 << l
            offs = [jnp.where(code == l, lo_h, 0.0) for lo_h in lows]
            if m < SUBLANES:
                us = [_dot(t, off) for t, off in zip(ts, offs)]
                yield
                ts = [t - _dot(u, t) for t, u in zip(ts, us)]
                yield
            else:
                lower = [jnp.concatenate([t[s + m:s + 2 * m, :] for s in range(0, c, 2 * m)], axis=0) for t in ts]
                us = [_dot(lo_rows, off) for lo_rows, off in zip(lower, offs)]
                yield
                new = [lo_rows - _dot(u, t) for lo_rows, u, t in zip(lower, us, ts)]
                ts = [jnp.concatenate(
                    [piece for j, s in enumerate(range(0, c, 2 * m))
                     for piece in (t[s:s + m, :], nw[j * m:(j + 1) * m, :])], axis=0) for t, nw in zip(ts, new)]
                yield
        sa = [(-jnp.where(m_a, _dot(ts[2 * i], x1[i]), _dot(ts[2 * i + 1], x1[i]))).astype(BF16) for i in range(n)]
        vs = [jnp.concatenate([ch[i]["v"], sa[i]], axis=0) for i in range(n)]
        yield
        a34 = [jnp.concatenate([jnp.where(incl, gs[i][(2 + hh) * c:(3 + hh) * c, 0:c], 0.0),
                                jnp.where(incl, gs[i][(2 + hh) * c:(3 + hh) * c, c:2 * c], 0.0)], axis=1)
               for i, hh in heads]
        yh = [_dot(a34[j], vs[j // 2]) for j in range(len(heads))]
        yield
        for i, (b, p) in enumerate(pairs):
            zed = jnp.concatenate([ch[i]["k_end"], ch[i]["b_end"]], axis=0)
            st_ref[b, p] = sts[i] * ch[i]["w_end"] + jnp.where(same_head, _dot(vs[i], zed, TN), 0.0)
            ys[b, p] = _dot(ch[i]["r_abs"], sts[i], NT) + jnp.where(m_a, yh[2 * i], yh[2 * i + 1])
        yield

    def finish(b, p, y, d):
        sl = sls[p]
        inv_n = 1.0 / RW_HEAD
        y_sum = head_sum(y)
        yield
        yc = y - y_sum * inv_n
        yc_sq = (yc * yc).astype(BF16)
        yield
        var = lax.dot_general(yc_sq, bd, NN, preferred_element_type=F32) * inv_n
        yield
        yn = yc * lax.rsqrt(var + GN_EPS) * par_ref[5:6, sl] + par_ref[6:7, sl]
        o_ref[b, :, sl] = ((yn + d["bonus"]) * d["gate"]).astype(o_ref.dtype)

    def run_all(gens):
        gens = list(gens)
        while gens:
            gens = [g for g in gens if next(g, StopIteration) is not StopIteration]

    group = 2 if n_batch % 2 == 0 else 1
    groups = [[(b, p) for b in range(g0, g0 + group) for p in range(n_pairs)] for g0 in range(0, n_batch, group)]
    pre, ys = {}, {}
    run_all(prep(*bp, pre) for bp in groups[0])
    for gi, pairs in enumerate(groups):
        fill = [prep(*bp, pre) for bp in (groups[gi + 1] if gi + 1 < len(groups) else [])]
        fill += [finish(*bp, ys[bp], pre[bp]) for bp in (groups[gi - 1] if gi > 0 else [])]
        n_stages = 2 * LEVELS + 2
        start = [(j * (n_stages - 4)) // max(len(fill), 1) for j in range(len(fill))]
        active = []
        for s, _ in enumerate(recur(pairs, pre, ys)):
            active += [g for j, g in enumerate(fill) if start[j] == s]
            active = [g for g in active if next(g, StopIteration) is not StopIteration]
        run_all(active)
    run_all(finish(*bp, ys[bp], pre[bp]) for bp in groups[-1])


def _rwkv_call(pf3, pb3, mu4, mu_lo, par, wup, aup, vup, bd, tri, code, v_first, col0, lo_col):
    batch, seq, _ = pf3.shape
    w = par.shape[-1]
    wl = mu_lo.shape[-1]
    first_layer = v_first is None
    col = lambda j: (lambda t: (0, t, j))
    const = lambda t: (0, 0)
    blk = (batch, CHUNK, w)
    row_spec = pl.BlockSpec(blk, col(0))
    in_specs = [
        pl.BlockSpec(blk, col(col0)),
        pl.BlockSpec(blk, col(col0 + 1)),
        pl.BlockSpec(blk, col(col0 + 2)),
        pl.BlockSpec(blk, col(col0 + 3)),
        pl.BlockSpec((batch, CHUNK, wl), col(lo_col)),
        pl.BlockSpec(mu4.shape, const),
        pl.BlockSpec(mu_lo.shape, const),
        pl.BlockSpec(par.shape, const),
        pl.BlockSpec(wup.shape, const),
        pl.BlockSpec(aup.shape, const),
        pl.BlockSpec(vup.shape, const),
        pl.BlockSpec(bd.shape, const),
        pl.BlockSpec(tri.shape, const),
        pl.BlockSpec(code.shape, const),
    ]
    args = [pb3, pb3, pb3, pb3, pf3, mu4, mu_lo, par, wup, aup, vup, bd, tri, code]
    out_sds = jax.ShapeDtypeStruct((batch, seq, w), BF16)
    if first_layer:
        out_specs = [row_spec, row_spec]
        out_shape = [out_sds, jax.ShapeDtypeStruct((batch, seq, w), F32)]
    else:
        in_specs.append(row_spec)
        args.append(v_first)
        out_specs = row_spec
        out_shape = out_sds
    return pl.pallas_call(
        functools.partial(_rwkv_kernel, first_layer=first_layer),
        grid=(seq // CHUNK,),
        in_specs=in_specs,
        out_specs=out_specs,
        out_shape=out_shape,
        scratch_shapes=[
            pltpu.VMEM((batch, SUBLANES, w), F32),
            pltpu.VMEM((batch, SUBLANES, wl), F32),
            pltpu.VMEM((batch, w // LANES, LANES, LANES), F32),
        ],
        compiler_params=pltpu.CompilerParams(
            dimension_semantics=("arbitrary",), vmem_limit_bytes=VMEM_LIMIT),
        name="rwkv7_branch_first" if first_layer else "rwkv7_branch",
    )(*args)


def _out_kernel(x_ref, ohg_ref, orw_ref, ghg_ref, grw_ref, gate_ref, wa_ref, wb_ref, wo_ref,
                lng_ref, lnb_ref, o_ref, *, alpha):
    d = functools.partial(lax.dot_general, dimension_numbers=NN, preferred_element_type=F32)
    y_hg = d(ohg_ref[...].astype(BF16), wa_ref[...])
    y_rw = d(orw_ref[...].astype(BF16), wb_ref[...])
    merged = _sigmoid(ghg_ref[...].astype(F32)) * y_hg + _sigmoid(grw_ref[...].astype(F32)) * y_rw
    out = d(merged.astype(BF16), wo_ref[...])
    xn = alpha * x_ref[...] + (1.0 + gate_ref[0]) * out
    mu = jnp.mean(xn, axis=-1, keepdims=True)
    xc = xn - mu
    var = jnp.mean(xc * xc, axis=-1, keepdims=True)
    o_ref[...] = xc * lax.rsqrt(var + LN_EPS) * lng_ref[...] + lnb_ref[...]


def _out_call(x2, o_hg, o_rw, proj, gate, wa, wb, wo, ln_g, ln_b, seq, tm, gate_col, alpha):
    bt, d = x2.shape
    wh = o_hg.shape[1]
    per_b = seq // tm
    row = lambda i: (i, 0)
    const = lambda i: (0, 0)
    return pl.pallas_call(
        functools.partial(_out_kernel, alpha=alpha),
        grid=(bt // tm,),
        in_specs=[
            pl.BlockSpec((tm, d), row),
            pl.BlockSpec((tm, wh), row),
            pl.BlockSpec((tm, wh), row),
            pl.BlockSpec((tm, d), lambda i: (i, gate_col)),
            pl.BlockSpec((tm, d), lambda i: (i, gate_col + 1)),
            pl.BlockSpec((1, 1, d), lambda i: (i // per_b, 0, 0)),
            pl.BlockSpec(wa.shape, const),
            pl.BlockSpec(wb.shape, const),
            pl.BlockSpec(wo.shape, const),
            pl.BlockSpec((1, d), const),
            pl.BlockSpec((1, d), const),
        ],
        out_specs=pl.BlockSpec((tm, d), row),
        out_shape=jax.ShapeDtypeStruct((bt, d), F32),
        compiler_params=pltpu.CompilerParams(
            dimension_semantics=("parallel",), vmem_limit_bytes=VMEM_LIMIT),
        name="merge_out_proj_norm",
    )(x2, o_hg, o_rw, proj, proj, gate, wa, wb, wo, ln_g, ln_b)


def kernel(x, c, w_ada, b_ada, w_in, hg_lower_bounds, hg_norm_g, rw_mu, rw_w0, rw_w_up, rw_a0, rw_a_up,
           rw_k_k, rw_k_a, rw_r_k, rw_v0, rw_v_down, rw_v_up, rw_gn_g, rw_gn_b, w_branch_hg, w_branch_rw,
           w_out, ln_g, ln_b):
    batch, seq, d = x.shape
    depth = w_in.shape[0]
    hgw = hg_norm_g.shape[1]
    rww = rw_w0.shape[1]
    r_decay = rw_w_up.shape[1]
    r_icl = rw_a_up.shape[1]
    r_vres = rw_v_up.shape[1]
    hg_cols = 4 * hgw
    rw_main = 4 * rww
    lo_w = 2 * LANES
    assert r_decay + r_icl + r_vres <= lo_w and hgw == rww and d == 2 * hgw
    assert seq % CHUNK == 0
    alpha = float((2 * depth) ** 0.25)

    lb_soft = jax.nn.softmax(hg_lower_bounds.astype(F32), axis=0)
    lower_bounds = jnp.cumsum(lb_soft, axis=0) - lb_soft[0]

    seg = jnp.asarray(_hgrn_segment_matrix(CHUNK), BF16)
    code = jnp.asarray(_split_code(CHUNK), jnp.int32)
    tri = jnp.asarray(np.tril(np.ones((CHUNK, CHUNK), np.float32)), BF16)
    bd = jnp.asarray(_block_ones(LANES, RW_HEAD), BF16)

    c_pad = jnp.zeros((8, d), F32).at[:batch].set(c)
    cond = _ada_call(c_pad, w_ada, b_ada)

    x2 = x.reshape(batch * seq, d)
    tm = min(1024, seq)
    v_first = None
    for l in range(depth):
        shift = cond[l, :batch, 0:d].reshape(batch, 1, d)
        scale = cond[l, :batch, d:2 * d].reshape(batch, 1, d)
        gate = cond[l, :batch, 2 * d:3 * d].reshape(batch, 1, d)

        wl = w_in[l]
        w_q, w_f, w_i, w_z = (wl[:, j * hgw:(j + 1) * hgw] for j in range(4))
        w_rw = wl[:, hg_cols:hg_cols + rw_main]
        w_lo = wl[:, hg_cols + rw_main:hg_cols + rw_main + r_decay + r_icl]
        w_gate = wl[:, hg_cols + rw_main + r_decay + r_icl:]
        w_vd = rw_v_down[l - 1] if l > 0 else jnp.zeros((d, r_vres), F32)
        pad = jnp.zeros((d, lo_w - (r_decay + r_icl + r_vres)), F32)
        w_cat = jnp.concatenate([w_q, w_f, w_lo, w_vd, pad, w_i, w_z, w_gate, w_rw], axis=1).astype(BF16)
        tn = 2 * hgw + lo_w
        assert (w_cat.shape[1] - tn) % tn == 0 and tn % LANES == 0
        proj_f, proj_b = _proj_call(x2, shift, scale, w_cat, seq, tm, tn)

        pf3 = proj_f.reshape(batch, seq, tn)
        pb3 = proj_b.reshape(batch, seq, w_cat.shape[1] - tn)
        o_hg = _hgrn_call(pf3, pb3, lower_bounds[l].reshape(1, hgw), hg_norm_g[l].reshape(1, hgw), seg, code)

        mu_l = rw_mu[l]
        mu4 = jnp.zeros((8, rww), F32).at[0:4].set(mu_l[:rw_main].reshape(4, rww))
        mu_lo = jnp.zeros((1, lo_w), F32).at[0, :r_decay + r_icl].set(mu_l[rw_main:])
        v0 = rw_v0[l - 1] if l > 0 else jnp.zeros((rww,), F32)
        par = jnp.stack([rw_w0[l], rw_a0[l], rw_k_k[l], rw_k_a[l], rw_r_k[l].reshape(rww),
                         rw_gn_g[l], rw_gn_b[l], v0], axis=0)
        wup = jnp.zeros((lo_w, rww), F32).at[0:r_decay].set(rw_w_up[l]).astype(BF16)
        aup = jnp.zeros((lo_w, rww), F32).at[r_decay:r_decay + r_icl].set(rw_a_up[l]).astype(BF16)
        vup_l = rw_v_up[l - 1] if l > 0 else jnp.zeros((r_vres, rww), F32)
        vup = jnp.zeros((lo_w, rww), F32).at[r_decay + r_icl:r_decay + r_icl + r_vres].set(vup_l).astype(BF16)
        col0 = (2 * hgw + w_gate.shape[1]) // rww
        res = _rwkv_call(pf3, pb3, mu4, mu_lo, par, wup, aup, vup, bd, tri, code, v_first, col0, 2 * hgw // lo_w)
        if l == 0:
            o_rw, v_first = res
        else:
            o_rw = res

        x2 = _out_call(x2, o_hg.reshape(batch * seq, hgw), o_rw.reshape(batch * seq, rww), proj_b, gate,
                       w_branch_hg[l].astype(BF16), w_branch_rw[l].astype(BF16), w_out[l].astype(BF16),
                       ln_g[l].reshape(1, d), ln_b[l].reshape(1, d), seq, min(512, seq), 2 * hgw // d, alpha)
    return x2.reshape(batch, seq, d)
```

```python
import functools
import math

import numpy as np
import jax
import jax.numpy as jnp
from jax import lax
from jax.experimental import pallas as pl
from jax.experimental.pallas import tpu as pltpu

F32 = jnp.float32
BF16 = jnp.bfloat16

HG_DK = 128
RW_HEAD = 64
LN_EPS = 1e-5
RMS_EPS = 1e-6
GN_EPS = 64e-5
L2_EPS = 1e-12
LB_FLOOR = 1e-30

LANES = 128
SUBLANES = 8
CHUNK = 128
LEVELS = int(math.log2(CHUNK))
STEP_CHUNKS = 2
SMALL_LEVELS = int(math.log2(SUBLANES))
VMEM_LIMIT = 56 * 1024 * 1024

NT = (((1,), (1,)), ((), ()))
TN = (((0,), (0,)), ((), ()))
NN = (((1,), (0,)), ((), ()))


def _dot(a, b, dims=NN):
    return lax.dot_general(a.astype(BF16), b.astype(BF16), dims, preferred_element_type=F32)


def _split2(x):
    hi = x.astype(BF16)
    lo = (x - hi.astype(F32)).astype(BF16)
    return hi, lo


def _dot3(a, b):
    ah, al = _split2(a)
    bh, bl = _split2(b)
    d = functools.partial(lax.dot_general, dimension_numbers=NN, preferred_element_type=F32)
    return d(ah, bh) + (d(ah, bl) + d(al, bh))


def _sel_dot(m01, x):
    w = x.shape[-1]
    out = lax.dot_general(m01, jnp.concatenate(_split2(x), axis=-1), NN, preferred_element_type=F32)
    return out[:, 0:w] + out[:, w:2 * w]


def _sigmoid(x):
    return 1.0 / (1.0 + jnp.exp(-x))


def _silu(x):
    return x * _sigmoid(x)


def _exp_neg(x):
    return jnp.exp2(x * (-math.log2(math.e)))


def _sigmoid_pair(x):
    t = jnp.exp(-jnp.abs(x))
    big = 1.0 / (1.0 + t)
    small = t * big
    pos = x >= 0.0
    return jnp.where(pos, big, small), jnp.where(pos, small, big)


def _split_code(c):
    t = np.arange(c)[:, None]
    s = np.arange(c)[None, :]
    x = t ^ s
    code = np.where(x > 0, np.floor(np.log2(np.maximum(x, 1))).astype(np.int32), -1)
    code = np.where(s > t, -2, code)
    return code.astype(np.int32)


def _hgrn_segment_matrix(c):
    blocks = []
    j = np.arange(c)[None, :]
    t = np.arange(c)[:, None]
    for l in range(SMALL_LEVELS):
        m = 1 << l
        start = (t // (2 * m)) * (2 * m)
        mid = start + m - 1
        second = ((t // m) % 2) == 1
        blk = np.where(second, (j > mid) & (j <= t), (j > t) & (j <= mid))
        blocks.append(blk)
    blocks.append(j <= t)
    return np.concatenate(blocks, axis=0).astype(np.float32)


def _block_ones(n, blk):
    i = np.arange(n)
    return (i[:, None] // blk == i[None, :] // blk).astype(np.float32)


def _ada_kernel(c_ref, w_ref, b_ref, o_ref):
    sc = _silu(c_ref[...])
    o_ref[0] = _dot3(sc, w_ref[0]) + b_ref[0]


def _ada_call(c_pad, w_ada, b_ada):
    depth, d, d3 = w_ada.shape
    rows = c_pad.shape[0]
    tn = d
    return pl.pallas_call(
        _ada_kernel,
        grid=(depth, d3 // tn),
        in_specs=[
            pl.BlockSpec((rows, d), lambda l, j: (0, 0)),
            pl.BlockSpec((1, d, tn), lambda l, j: (l, 0, j)),
            pl.BlockSpec((1, 1, tn), lambda l, j: (l, 0, j)),
        ],
        out_specs=pl.BlockSpec((1, rows, tn), lambda l, j: (l, 0, j)),
        out_shape=jax.ShapeDtypeStruct((depth, rows, d3), F32),
        compiler_params=pltpu.CompilerParams(
            dimension_semantics=("parallel", "parallel"), vmem_limit_bytes=VMEM_LIMIT),
        name="adaln_cond",
    )(c_pad, w_ada, b_ada.reshape(depth, 1, d3))


def _proj_kernel(x_ref, shift_ref, scale_ref, w_ref, of_ref, ob_ref, h_ref):
    j = pl.program_id(1)

    @pl.when(j == 0)
    def _():
        h = x_ref[...] * (1.0 + scale_ref[0]) + shift_ref[0]
        h_ref[...] = h.astype(BF16)
        of_ref[...] = lax.dot_general(h_ref[...], w_ref[...], NN, preferred_element_type=F32)

    @pl.when(j > 0)
    def _():
        ob_ref[...] = lax.dot_general(h_ref[...], w_ref[...], NN, preferred_element_type=F32).astype(BF16)


def _proj_call(x2, shift, scale, w_cat, seq, tm, tn):
    bt, d = x2.shape
    n = w_cat.shape[1]
    per_b = seq // tm
    return pl.pallas_call(
        _proj_kernel,
        grid=(bt // tm, n // tn),
        in_specs=[
            pl.BlockSpec((tm, d), lambda i, j: (i, 0)),
            pl.BlockSpec((1, 1, d), lambda i, j: (i // per_b, 0, 0)),
            pl.BlockSpec((1, 1, d), lambda i, j: (i // per_b, 0, 0)),
            pl.BlockSpec((d, tn), lambda i, j: (0, j)),
        ],
        out_specs=[
            pl.BlockSpec((tm, tn), lambda i, j: (i, 0)),
            pl.BlockSpec((tm, tn), lambda i, j: (i, jnp.maximum(j - 1, 0))),
        ],
        out_shape=[jax.ShapeDtypeStruct((bt, tn), F32), jax.ShapeDtypeStruct((bt, n - tn), BF16)],
        scratch_shapes=[pltpu.VMEM((tm, d), BF16)],
        compiler_params=pltpu.CompilerParams(
            dimension_semantics=("parallel", "arbitrary"), vmem_limit_bytes=VMEM_LIMIT),
        name="modulate_in_proj",
    )(x2, shift, scale, w_cat)


def _hgrn_kernel(q_ref, f_ref, i_ref, z_ref, lb_ref, g_ref, seg_ref, code_ref, o_ref, st_ref, *, n_heads):
    c = CHUNK
    n_batch, rows, _ = q_ref.shape
    items = [(ci, b) for ci in range(rows // c) for b in range(n_batch)]
    rsl = lambda ci: slice(ci * c, (ci + 1) * c)

    @pl.when(pl.program_id(0) == 0)
    def _():
        st_ref[...] = jnp.zeros_like(st_ref)

    lb = lb_ref[...]
    lb_floor = jnp.maximum(lb, LB_FLOOR)
    code = code_ref[...]
    seg = seg_ref[...]
    g = g_ref[...]

    ks, d_alls = {}, {}
    for ci, b in items:
        sig, sig_neg = _sigmoid_pair(f_ref[b, rsl(ci), :])
        log_f = jnp.log(lb_floor + (1.0 - lb) * sig)
        ks[ci, b] = (1.0 - lb) * sig_neg
        d_alls[ci, b] = _sel_dot(seg, -log_f)

    sls = [slice(h * HG_DK, (h + 1) * HG_DK) for h in range(n_heads)]

    def scores(ci, b):
        d_all = d_alls[ci, b]
        nb = d_all[SMALL_LEVELS * c:(SMALL_LEVELS + 1) * c, :]
        e_last = _exp_neg(nb[c - 1:c, :])
        e_b = _exp_neg(nb).astype(BF16)
        e_suf = _exp_neg(nb[c - 1:c, :] - nb).astype(BF16)
        e_lvl = [_exp_neg(d_all[l * c:(l + 1) * c, :]).astype(BF16) for l in range(SMALL_LEVELS)]
        for l in range(SMALL_LEVELS, LEVELS):
            m = 1 << l
            ref = jnp.concatenate(
                [jnp.broadcast_to(nb[s + m - 1:s + m, :], (2 * m, nb.shape[1])) for s in range(0, c, 2 * m)], axis=0)
            e_lvl.append(_exp_neg(jnp.abs(nb - ref)).astype(BF16))
        q, k, v = q_ref[b, rsl(ci), :].astype(BF16), ks[ci, b].astype(BF16), i_ref[b, rsl(ci), :].astype(BF16)
        out = []
        for sl in sls:
            qh, kh = q[:, sl], k[:, sl]
            sc = jnp.where(code == -1, _dot(qh, kh, NT), 0.0)
            for l in range(LEVELS):
                e = e_lvl[l][:, sl]
                sc = jnp.where(code == l, _dot(qh * e, kh * e, NT), sc)
            out.append(dict(sc=sc, v=v[:, sl], q_in=qh * e_b[:, sl], k_out=kh * e_suf[:, sl], e_last=e_last[:, sl]))
        return out

    def finish(ci, b, heads):
        for h, (sl, d) in enumerate(zip(sls, heads)):
            st = st_ref[b, h]
            o = _dot(d["sc"], d["v"]) + _dot(d["q_in"], st, NT)
            st_ref[b, h] = st * d["e_last"] + _dot(d["v"], d["k_out"], TN)
            o = o * lax.rsqrt(jnp.mean(o * o, axis=-1, keepdims=True) + RMS_EPS)
            o_ref[b, rsl(ci), sl] = (o * g[:, sl] * _silu(z_ref[b, rsl(ci), sl].astype(F32))).astype(o_ref.dtype)

    pending = scores(*items[0])
    for prev, cur in zip(items[:-1], items[1:]):
        nxt = scores(*cur)
        finish(*prev, pending)
        pending = nxt
    finish(*items[-1], pending)


def _hgrn_call(pf3, pb3, lb, norm_g, seg, code):
    batch, seq, _ = pf3.shape
    w = lb.shape[-1]
    n_heads = w // HG_DK
    col = lambda j: (lambda t: (0, t, j))
    const = lambda t: (0, 0)
    blk = (batch, STEP_CHUNKS * CHUNK, w)
    return pl.pallas_call(
        functools.partial(_hgrn_kernel, n_heads=n_heads),
        grid=(seq // (STEP_CHUNKS * CHUNK),),
        in_specs=[
            pl.BlockSpec(blk, col(0)),
            pl.BlockSpec(blk, col(1)),
            pl.BlockSpec(blk, col(0)),
            pl.BlockSpec(blk, col(1)),
            pl.BlockSpec((1, w), const),
            pl.BlockSpec((1, w), const),
            pl.BlockSpec(seg.shape, const),
            pl.BlockSpec(code.shape, const),
        ],
        out_specs=pl.BlockSpec(blk, col(0)),
        out_shape=jax.ShapeDtypeStruct((batch, seq, w), BF16),
        scratch_shapes=[pltpu.VMEM((batch, n_heads, HG_DK, HG_DK), F32)],
        compiler_params=pltpu.CompilerParams(
            dimension_semantics=("arbitrary",), vmem_limit_bytes=VMEM_LIMIT),
        name="hgrn2_branch",
    )(pf3, pf3, pb3, pb3, lb, norm_g, seg, code)


def _rwkv_kernel(*refs, first_layer):
    (r_ref, k_ref, v_ref, z_ref, lo_ref, mu_ref, mulo_ref, par_ref, wup_ref, aup_ref, vup_ref,
     bd_ref, tri_ref, code_ref) = refs[:14]
    if first_layer:
        vfirst_ref, (o_ref, vout_ref) = None, refs[14:16]
    else:
        vfirst_ref, o_ref, vout_ref = refs[14], refs[15], None
    prev_ref, prevlo_ref, st_ref = refs[16:]
    c = CHUNK
    n_batch, rows, w = r_ref.shape
    n_chunks = rows // c
    n_pairs = w // LANES

    @pl.when(pl.program_id(0) == 0)
    def _():
        prev_ref[...] = jnp.zeros_like(prev_ref)
        prevlo_ref[...] = jnp.zeros_like(prevlo_ref)
        st_ref[...] = jnp.zeros_like(st_ref)

    row0 = lax.broadcasted_iota(jnp.int32, (c, 1), 0) == 0
    bd = bd_ref[...]
    code = code_ref[...]
    tri = tri_ref[...]
    lane = lax.broadcasted_iota(jnp.int32, (1, LANES), 1)
    m_a = lane < RW_HEAD
    row_i = lax.broadcasted_iota(jnp.int32, (LANES, LANES), 0)
    col_i = lax.broadcasted_iota(jnp.int32, (LANES, LANES), 1)
    same_head = (row_i < RW_HEAD) == (col_i < RW_HEAD)
    strict = code >= 0
    incl = code >= -1
    eye = jnp.where(code == -1, 1.0, 0.0)
    sls = [slice(p * LANES, (p + 1) * LANES) for p in range(n_pairs)]

    def head_sum(x):
        return _dot(x, bd)

    def shifted(p, prev_row, m):
        prev = jnp.where(row0, prev_row, pltpu.roll(p, 1, 0))
        return p + (prev - p) * m

    def load_shifted(ref, ci, b, sl, carry_ref, carry_row, m):
        rs = slice(ci * c, (ci + 1) * c)
        cur = ref[b, rs, sl].astype(F32)
        if ci == 0:
            prev_row = carry_ref[b, carry_row:carry_row + 1, sl]
        else:
            prev_row = ref[b, ci * c - 1:ci * c, sl].astype(F32)
        if ci == n_chunks - 1:
            carry_ref[b, carry_row:carry_row + 1, sl] = cur[c - 1:c, :]
        return shifted(cur, prev_row, m)

    lo_cache = {}

    def low_rank_inputs(ci, b):
        if (ci, b) not in lo_cache:
            lo = load_shifted(lo_ref, ci, b, slice(None), prevlo_ref, 0, mulo_ref[...])
            lo_cache[ci, b] = (jnp.tanh(lo).astype(BF16), lo.astype(BF16))
        return lo_cache[ci, b]

    def prep(ci, b, p, pre):
        sl = sls[p]
        w0, a0, k_k, k_a, r_k, _, _, v0 = (par_ref[i:i + 1, sl] for i in range(8))
        rs = slice(ci * c, (ci + 1) * c)
        tanh_lo, lo = low_rank_inputs(ci, b)
        r, k, v, z = (load_shifted(ref, ci, b, sl, prev_ref, i, mu_ref[i:i + 1, sl])
                      for i, ref in enumerate((r_ref, k_ref, v_ref, z_ref)))
        gate = _silu(z)
        kk = k * k_k
        kk_sq = (kk * kk).astype(BF16)
        yield
        w_lin = lax.dot_general(tanh_lo, wup_ref[:, sl], NN, preferred_element_type=F32)
        a_lin = lax.dot_general(lo, aup_ref[:, sl], NN, preferred_element_type=F32)
        if not first_layer:
            v_lin = lax.dot_general(lo, vup_ref[:, sl], NN, preferred_element_type=F32)
        kk_ss = lax.dot_general(kk_sq, bd, NN, preferred_element_type=F32)
        yield
        lw = -math.exp(-0.5) * _sigmoid(w0 + w_lin)
        a = _sigmoid(a0 + a_lin)
        if first_layer:
            vout_ref[b, rs, sl] = v
        else:
            v = v + (vfirst_ref[b, rs, sl] - v) * _sigmoid(v0 + v_lin)
        kk = kk / jnp.maximum(jnp.sqrt(kk_ss), L2_EPS)
        k = k * (1.0 + (a - 1.0) * k_a)
        be = kk * a
        rkr = (r * k * r_k).astype(BF16)
        lw_parts = jnp.concatenate(_split2(lw), axis=-1)
        yield
        bonus_sum = lax.dot_general(rkr, bd, NN, preferred_element_type=F32)
        cum2 = lax.dot_general(tri, lw_parts, NN, preferred_element_type=F32)
        yield
        bonus = bonus_sum * v
        cum = cum2[:, 0:LANES] + cum2[:, LANES:2 * LANES]
        cum_prev = cum - lw
        c_mid = cum[c // 2 - 1:c // 2, :]
        c_end = cum[c - 1:c, :]
        e_mid_bwd = jnp.exp(c_mid - cum)
        e_end = jnp.exp(c_end - cum)
        pre[ci, b, p] = dict(
            r_abs=(r * jnp.exp(cum)).astype(BF16), kn_abs=(kk * jnp.exp(cum_prev)).astype(BF16),
            r_mid=(r * jnp.exp(cum - c_mid)).astype(BF16), kn_mid=(kk * jnp.exp(cum_prev - c_mid)).astype(BF16),
            k_mid=(k * e_mid_bwd).astype(BF16), b_mid=(be * e_mid_bwd).astype(BF16),
            k_end=(k * e_end).astype(BF16), b_end=(be * e_end).astype(BF16),
            w_end=jnp.exp(c_end), v=v.astype(BF16), bonus=bonus, gate=gate)

    def recur(pairs, pre, ys):
        n = len(pairs)
        heads = [(i, hh) for i in range(n) for hh in range(2)]
        ch = [pre[bp] for bp in pairs]
        gs = []
        for d in ch:
            zero = jnp.zeros_like(d["kn_mid"])
            lhs = jnp.concatenate([jnp.where(m_a, d["kn_mid"], zero), jnp.where(m_a, zero, d["kn_mid"]),
                                   jnp.where(m_a, d["r_mid"], zero), jnp.where(m_a, zero, d["r_mid"])], axis=0)
            rhs = jnp.concatenate([d["k_mid"], d["b_mid"]], axis=0)
            gs.append(_dot(lhs, rhs, NT))
        yield
        sts = [st_ref[b, p] for _, b, p in pairs]
        lows = [jnp.where(strict, gs[i][hh * c:(hh + 1) * c, c:2 * c], 0.0) for i, hh in heads]
        a1v = [_dot(jnp.where(strict, gs[i][hh * c:(hh + 1) * c, 0:c], 0.0), ch[i]["v"]) for i, hh in heads]
        x1 = [_dot(ch[i]["kn_abs"], sts[i], NT) + jnp.where(m_a, a1v[2 * i], a1v[2 * i + 1]) for i in range(n)]
        yield
        ts = [eye - jnp.where(code == 0, lo_h, 0.0) for lo_h in lows]
        for l in range(1, LEVELS):
            m = 1 << l
            offs = [jnp.where(code == l, lo_h, 0.0) for lo_h in lows]
            if m < SUBLANES:
                us = [_dot(t, off) for t, off in zip(ts, offs)]
                yield
                ts = [t - _dot(u, t) for t, u in zip(ts, us)]
                yield
            else:
                lower = [jnp.concatenate([t[s + m:s + 2 * m, :] for s in range(0, c, 2 * m)], axis=0) for t in ts]
                us = [_dot(lo_rows, off) for lo_rows, off in zip(lower, offs)]
                yield
                new = [lo_rows - _dot(u, t) for lo_rows, u, t in zip(lower, us, ts)]
                ts = [jnp.concatenate(
                    [piece for j, s in enumerate(range(0, c, 2 * m))
                     for piece in (t[s:s + m, :], nw[j * m:(j + 1) * m, :])], axis=0) for t, nw in zip(ts, new)]
                yield
        sa = [(-jnp.where(m_a, _dot(ts[2 * i], x1[i]), _dot(ts[2 * i + 1], x1[i]))).astype(BF16) for i in range(n)]
        vs = [jnp.concatenate([ch[i]["v"], sa[i]], axis=0) for i in range(n)]
        yield
        a34 = [jnp.concatenate([jnp.where(incl, gs[i][(2 + hh) * c:(3 + hh) * c, 0:c], 0.0),
                                jnp.where(incl, gs[i][(2 + hh) * c:(3 + hh) * c, c:2 * c], 0.0)], axis=1)
               for i, hh in heads]
        yh = [_dot(a34[j], vs[j // 2]) for j in range(len(heads))]
        yield
        for i, (ci, b, p) in enumerate(pairs):
            zed = jnp.concatenate([ch[i]["k_end"], ch[i]["b_end"]], axis=0)
            st_ref[b, p] = sts[i] * ch[i]["w_end"] + jnp.where(same_head, _dot(vs[i], zed, TN), 0.0)
            ys[ci, b, p] = _dot(ch[i]["r_abs"], sts[i], NT) + jnp.where(m_a, yh[2 * i], yh[2 * i + 1])
        yield

    def finish(ci, b, p, y, d):
        sl = sls[p]
        inv_n = 1.0 / RW_HEAD
        y_sum = head_sum(y)
        yield
        yc = y - y_sum * inv_n
        yc_sq = (yc * yc).astype(BF16)
        yield
        var = lax.dot_general(yc_sq, bd, NN, preferred_element_type=F32) * inv_n
        yield
        yn = yc * lax.rsqrt(var + GN_EPS) * par_ref[5:6, sl] + par_ref[6:7, sl]
        o_ref[b, ci * c:(ci + 1) * c, sl] = ((yn + d["bonus"]) * d["gate"]).astype(o_ref.dtype)

    def run_all(gens):
        gens = list(gens)
        while gens:
            gens = [g for g in gens if next(g, StopIteration) is not StopIteration]

    group = 2 if n_batch % 2 == 0 else 1
    groups = [[(ci, b, p) for b in range(g0, g0 + group) for p in range(n_pairs)]
              for ci in range(n_chunks) for g0 in range(0, n_batch, group)]
    pre, ys = {}, {}
    run_all(prep(*bp, pre) for bp in groups[0])
    for gi, pairs in enumerate(groups):
        fill = [prep(*bp, pre) for bp in (groups[gi + 1] if gi + 1 < len(groups) else [])]
        fill += [finish(*bp, ys[bp], pre[bp]) for bp in (groups[gi - 1] if gi > 0 else [])]
        n_stages = 2 * LEVELS + 2
        start = [(j * (n_stages - 4)) // max(len(fill), 1) for j in range(len(fill))]
        active = []
        for s, _ in enumerate(recur(pairs, pre, ys)):
            active += [g for j, g in enumerate(fill) if start[j] == s]
            active = [g for g in active if next(g, StopIteration) is not StopIteration]
        run_all(active)
    run_all(finish(*bp, ys[bp], pre[bp]) for bp in groups[-1])


def _rwkv_call(pf3, pb3, mu4, mu_lo, par, wup, aup, vup, bd, tri, code, v_first, col0, lo_col):
    batch, seq, _ = pf3.shape
    w = par.shape[-1]
    wl = mu_lo.shape[-1]
    first_layer = v_first is None
    col = lambda j: (lambda t: (0, t, j))
    const = lambda t: (0, 0)
    blk = (batch, STEP_CHUNKS * CHUNK, w)
    row_spec = pl.BlockSpec(blk, col(0))
    in_specs = [
        pl.BlockSpec(blk, col(col0)),
        pl.BlockSpec(blk, col(col0 + 1)),
        pl.BlockSpec(blk, col(col0 + 2)),
        pl.BlockSpec(blk, col(col0 + 3)),
        pl.BlockSpec((batch, STEP_CHUNKS * CHUNK, wl), col(lo_col)),
        pl.BlockSpec(mu4.shape, const),
        pl.BlockSpec(mu_lo.shape, const),
        pl.BlockSpec(par.shape, const),
        pl.BlockSpec(wup.shape, const),
        pl.BlockSpec(aup.shape, const),
        pl.BlockSpec(vup.shape, const),
        pl.BlockSpec(bd.shape, const),
        pl.BlockSpec(tri.shape, const),
        pl.BlockSpec(code.shape, const),
    ]
    args = [pb3, pb3, pb3, pb3, pf3, mu4, mu_lo, par, wup, aup, vup, bd, tri, code]
    out_sds = jax.ShapeDtypeStruct((batch, seq, w), BF16)
    if first_layer:
        out_specs = [row_spec, row_spec]
        out_shape = [out_sds, jax.ShapeDtypeStruct((batch, seq, w), F32)]
    else:
        in_specs.append(row_spec)
        args.append(v_first)
        out_specs = row_spec
        out_shape = out_sds
    return pl.pallas_call(
        functools.partial(_rwkv_kernel, first_layer=first_layer),
        grid=(seq // (STEP_CHUNKS * CHUNK),),
        in_specs=in_specs,
        out_specs=out_specs,
        out_shape=out_shape,
        scratch_shapes=[
            pltpu.VMEM((batch, SUBLANES, w), F32),
            pltpu.VMEM((batch, SUBLANES, wl), F32),
            pltpu.VMEM((batch, w // LANES, LANES, LANES), F32),
        ],
        compiler_params=pltpu.CompilerParams(
            dimension_semantics=("arbitrary",), vmem_limit_bytes=VMEM_LIMIT),
        name="rwkv7_branch_first" if first_layer else "rwkv7_branch",
    )(*args)


def _out_kernel(x_ref, ohg_ref, orw_ref, ghg_ref, grw_ref, gate_ref, wa_ref, wb_ref, wo_ref,
                lng_ref, lnb_ref, o_ref, *, alpha):
    d = functools.partial(lax.dot_general, dimension_numbers=NN, preferred_element_type=F32)
    y_hg = d(ohg_ref[...].astype(BF16), wa_ref[...])
    y_rw = d(orw_ref[...].astype(BF16), wb_ref[...])
    merged = _sigmoid(ghg_ref[...].astype(F32)) * y_hg + _sigmoid(grw_ref[...].astype(F32)) * y_rw
    out = d(merged.astype(BF16), wo_ref[...])
    xn = alpha * x_ref[...] + (1.0 + gate_ref[0]) * out
    mu = jnp.mean(xn, axis=-1, keepdims=True)
    xc = xn - mu
    var = jnp.mean(xc * xc, axis=-1, keepdims=True)
    o_ref[...] = xc * lax.rsqrt(var + LN_EPS) * lng_ref[...] + lnb_ref[...]


def _out_call(x2, o_hg, o_rw, proj, gate, wa, wb, wo, ln_g, ln_b, seq, tm, gate_col, alpha):
    bt, d = x2.shape
    wh = o_hg.shape[1]
    per_b = seq // tm
    row = lambda i: (i, 0)
    const = lambda i: (0, 0)
    return pl.pallas_call(
        functools.partial(_out_kernel, alpha=alpha),
        grid=(bt // tm,),
        in_specs=[
            pl.BlockSpec((tm, d), row),
            pl.BlockSpec((tm, wh), row),
            pl.BlockSpec((tm, wh), row),
            pl.BlockSpec((tm, d), lambda i: (i, gate_col)),
            pl.BlockSpec((tm, d), lambda i: (i, gate_col + 1)),
            pl.BlockSpec((1, 1, d), lambda i: (i // per_b, 0, 0)),
            pl.BlockSpec(wa.shape, const),
            pl.BlockSpec(wb.shape, const),
            pl.BlockSpec(wo.shape, const),
            pl.BlockSpec((1, d), const),
            pl.BlockSpec((1, d), const),
        ],
        out_specs=pl.BlockSpec((tm, d), row),
        out_shape=jax.ShapeDtypeStruct((bt, d), F32),
        compiler_params=pltpu.CompilerParams(
            dimension_semantics=("parallel",), vmem_limit_bytes=VMEM_LIMIT),
        name="merge_out_proj_norm",
    )(x2, o_hg, o_rw, proj, proj, gate, wa, wb, wo, ln_g, ln_b)


def kernel(x, c, w_ada, b_ada, w_in, hg_lower_bounds, hg_norm_g, rw_mu, rw_w0, rw_w_up, rw_a0, rw_a_up,
           rw_k_k, rw_k_a, rw_r_k, rw_v0, rw_v_down, rw_v_up, rw_gn_g, rw_gn_b, w_branch_hg, w_branch_rw,
           w_out, ln_g, ln_b):
    batch, seq, d = x.shape
    depth = w_in.shape[0]
    hgw = hg_norm_g.shape[1]
    rww = rw_w0.shape[1]
    r_decay = rw_w_up.shape[1]
    r_icl = rw_a_up.shape[1]
    r_vres = rw_v_up.shape[1]
    hg_cols = 4 * hgw
    rw_main = 4 * rww
    lo_w = 2 * LANES
    assert r_decay + r_icl + r_vres <= lo_w and hgw == rww and d == 2 * hgw
    assert seq % (STEP_CHUNKS * CHUNK) == 0
    alpha = float((2 * depth) ** 0.25)

    lb_soft = jax.nn.softmax(hg_lower_bounds.astype(F32), axis=0)
    lower_bounds = jnp.cumsum(lb_soft, axis=0) - lb_soft[0]

    seg = jnp.asarray(_hgrn_segment_matrix(CHUNK), BF16)
    code = jnp.asarray(_split_code(CHUNK), jnp.int32)
    tri = jnp.asarray(np.tril(np.ones((CHUNK, CHUNK), np.float32)), BF16)
    bd = jnp.asarray(_block_ones(LANES, RW_HEAD), BF16)

    c_pad = jnp.zeros((8, d), F32).at[:batch].set(c)
    cond = _ada_call(c_pad, w_ada, b_ada)

    x2 = x.reshape(batch * seq, d)
    tm = min(1024, seq)
    v_first = None
    for l in range(depth):
        shift = cond[l, :batch, 0:d].reshape(batch, 1, d)
        scale = cond[l, :batch, d:2 * d].reshape(batch, 1, d)
        gate = cond[l, :batch, 2 * d:3 * d].reshape(batch, 1, d)

        wl = w_in[l]
        w_q, w_f, w_i, w_z = (wl[:, j * hgw:(j + 1) * hgw] for j in range(4))
        w_rw = wl[:, hg_cols:hg_cols + rw_main]
        w_lo = wl[:, hg_cols + rw_main:hg_cols + rw_main + r_decay + r_icl]
        w_gate = wl[:, hg_cols + rw_main + r_decay + r_icl:]
        w_vd = rw_v_down[l - 1] if l > 0 else jnp.zeros((d, r_vres), F32)
        pad = jnp.zeros((d, lo_w - (r_decay + r_icl + r_vres)), F32)
        w_cat = jnp.concatenate([w_q, w_f, w_lo, w_vd, pad, w_i, w_z, w_gate, w_rw], axis=1).astype(BF16)
        tn = 2 * hgw + lo_w
        assert (w_cat.shape[1] - tn) % tn == 0 and tn % LANES == 0
        proj_f, proj_b = _proj_call(x2, shift, scale, w_cat, seq, tm, tn)

        pf3 = proj_f.reshape(batch, seq, tn)
        pb3 = proj_b.reshape(batch, seq, w_cat.shape[1] - tn)
        o_hg = _hgrn_call(pf3, pb3, lower_bounds[l].reshape(1, hgw), hg_norm_g[l].reshape(1, hgw), seg, code)

        mu_l = rw_mu[l]
        mu4 = jnp.zeros((8, rww), F32).at[0:4].set(mu_l[:rw_main].reshape(4, rww))
        mu_lo = jnp.zeros((1, lo_w), F32).at[0, :r_decay + r_icl].set(mu_l[rw_main:])
        v0 = rw_v0[l - 1] if l > 0 else jnp.zeros((rww,), F32)
        par = jnp.stack([rw_w0[l], rw_a0[l], rw_k_k[l], rw_k_a[l], rw_r_k[l].reshape(rww),
                         rw_gn_g[l], rw_gn_b[l], v0], axis=0)
        wup = jnp.zeros((lo_w, rww), F32).at[0:r_decay].set(rw_w_up[l]).astype(BF16)
        aup = jnp.zeros((lo_w, rww), F32).at[r_decay:r_decay + r_icl].set(rw_a_up[l]).astype(BF16)
        vup_l = rw_v_up[l - 1] if l > 0 else jnp.zeros((r_vres, rww), F32)
        vup = jnp.zeros((lo_w, rww), F32).at[r_decay + r_icl:r_decay + r_icl + r_vres].set(vup_l).astype(BF16)
        col0 = (2 * hgw + w_gate.shape[1]) // rww
        res = _rwkv_call(pf3, pb3, mu4, mu_lo, par, wup, aup, vup, bd, tri, code, v_first, col0, 2 * hgw // lo_w)
        if l == 0:
            o_rw, v_first = res
        else:
            o_rw = res

        x2 = _out_call(x2, o_hg.reshape(batch * seq, hgw), o_rw.reshape(batch * seq, rww), proj_b, gate,
                       w_branch_hg[l].astype(BF16), w_branch_rw[l].astype(BF16), w_out[l].astype(BF16),
                       ln_g[l].reshape(1, d), ln_b[l].reshape(1, d), seq, tm, 2 * hgw // d, alpha)
    return x2.reshape(batch, seq, d)
```

```python
import functools
import math

import numpy as np
import jax
import jax.numpy as jnp
from jax import lax
from jax.experimental import pallas as pl
from jax.experimental.pallas import tpu as pltpu

F32 = jnp.float32
BF16 = jnp.bfloat16

HG_DK = 128
RW_HEAD = 64
LN_EPS = 1e-5
RMS_EPS = 1e-6
GN_EPS = 64e-5
L2_EPS = 1e-12
LB_FLOOR = 1e-30

LANES = 128
SUBLANES = 8
CHUNK = 128
LEVELS = int(math.log2(CHUNK))
STEP_CHUNKS = 2
SMALL_LEVELS = int(math.log2(SUBLANES))
VMEM_LIMIT = 56 * 1024 * 1024

NT = (((1,), (1,)), ((), ()))
TN = (((0,), (0,)), ((), ()))
NN = (((1,), (0,)), ((), ()))


def _dot(a, b, dims=NN):
    return lax.dot_general(a.astype(BF16), b.astype(BF16), dims, preferred_element_type=F32)


def _split2(x):
    hi = x.astype(BF16)
    lo = (x - hi.astype(F32)).astype(BF16)
    return hi, lo


def _dot3(a, b):
    ah, al = _split2(a)
    bh, bl = _split2(b)
    d = functools.partial(lax.dot_general, dimension_numbers=NN, preferred_element_type=F32)
    return d(ah, bh) + (d(ah, bl) + d(al, bh))


def _sel_dot(m01, x):
    w = x.shape[-1]
    out = lax.dot_general(m01, jnp.concatenate(_split2(x), axis=-1), NN, preferred_element_type=F32)
    return out[:, 0:w] + out[:, w:2 * w]


def _sigmoid(x):
    return 1.0 / (1.0 + jnp.exp(-x))


def _silu(x):
    return x * _sigmoid(x)


def _exp_neg(x):
    return jnp.exp2(x * (-math.log2(math.e)))


def _sigmoid_pair(x):
    t = jnp.exp(-jnp.abs(x))
    big = 1.0 / (1.0 + t)
    small = t * big
    pos = x >= 0.0
    return jnp.where(pos, big, small), jnp.where(pos, small, big)


def _split_code(c):
    t = np.arange(c)[:, None]
    s = np.arange(c)[None, :]
    x = t ^ s
    code = np.where(x > 0, np.floor(np.log2(np.maximum(x, 1))).astype(np.int32), -1)
    code = np.where(s > t, -2, code)
    return code.astype(np.int32)


def _hgrn_segment_matrix(c):
    blocks = []
    j = np.arange(c)[None, :]
    t = np.arange(c)[:, None]
    for l in range(SMALL_LEVELS):
        m = 1 << l
        start = (t // (2 * m)) * (2 * m)
        mid = start + m - 1
        second = ((t // m) % 2) == 1
        blk = np.where(second, (j > mid) & (j <= t), (j > t) & (j <= mid))
        blocks.append(blk)
    blocks.append(j <= t)
    return np.concatenate(blocks, axis=0).astype(np.float32)


def _block_ones(n, blk):
    i = np.arange(n)
    return (i[:, None] // blk == i[None, :] // blk).astype(np.float32)


def _ada_kernel(c_ref, w_ref, b_ref, o_ref):
    sc = _silu(c_ref[...])
    o_ref[0] = _dot3(sc, w_ref[0]) + b_ref[0]


def _ada_call(c_pad, w_ada, b_ada):
    depth, d, d3 = w_ada.shape
    rows = c_pad.shape[0]
    tn = d
    return pl.pallas_call(
        _ada_kernel,
        grid=(depth, d3 // tn),
        in_specs=[
            pl.BlockSpec((rows, d), lambda l, j: (0, 0)),
            pl.BlockSpec((1, d, tn), lambda l, j: (l, 0, j)),
            pl.BlockSpec((1, 1, tn), lambda l, j: (l, 0, j)),
        ],
        out_specs=pl.BlockSpec((1, rows, tn), lambda l, j: (l, 0, j)),
        out_shape=jax.ShapeDtypeStruct((depth, rows, d3), F32),
        compiler_params=pltpu.CompilerParams(
            dimension_semantics=("parallel", "parallel"), vmem_limit_bytes=VMEM_LIMIT),
        name="adaln_cond",
    )(c_pad, w_ada, b_ada.reshape(depth, 1, d3))


def _proj_kernel(x_ref, shift_ref, scale_ref, wf_ref, wb_ref, of_ref, ob_ref, h_ref):
    j = pl.program_id(1)

    @pl.when(j == 0)
    def _():
        h = x_ref[...] * (1.0 + scale_ref[0]) + shift_ref[0]
        h_ref[...] = h.astype(BF16)
        of_ref[...] = lax.dot_general(h_ref[...], wf_ref[...], NN, preferred_element_type=F32)

    @pl.when(j > 0)
    def _():
        ob_ref[...] = lax.dot_general(h_ref[...], wb_ref[...], NN, preferred_element_type=F32).astype(BF16)


def _proj_call(x2, shift, scale, w_f, w_b, seq, tm, tn_b):
    bt, d = x2.shape
    n_f, n_b = w_f.shape[1], w_b.shape[1]
    per_b = seq // tm
    b_tile = lambda i, j: (i, jnp.maximum(j - 1, 0))
    return pl.pallas_call(
        _proj_kernel,
        grid=(bt // tm, 1 + n_b // tn_b),
        in_specs=[
            pl.BlockSpec((tm, d), lambda i, j: (i, 0)),
            pl.BlockSpec((1, 1, d), lambda i, j: (i // per_b, 0, 0)),
            pl.BlockSpec((1, 1, d), lambda i, j: (i // per_b, 0, 0)),
            pl.BlockSpec((d, n_f), lambda i, j: (0, 0)),
            pl.BlockSpec((d, tn_b), lambda i, j: (0, jnp.maximum(j - 1, 0))),
        ],
        out_specs=[pl.BlockSpec((tm, n_f), lambda i, j: (i, 0)), pl.BlockSpec((tm, tn_b), b_tile)],
        out_shape=[jax.ShapeDtypeStruct((bt, n_f), F32), jax.ShapeDtypeStruct((bt, n_b), BF16)],
        scratch_shapes=[pltpu.VMEM((tm, d), BF16)],
        compiler_params=pltpu.CompilerParams(
            dimension_semantics=("parallel", "arbitrary"), vmem_limit_bytes=VMEM_LIMIT),
        name="modulate_in_proj",
    )(x2, shift, scale, w_f, w_b)


def _hgrn_kernel(q_ref, f_ref, i_ref, z_ref, lb_ref, g_ref, seg_ref, code_ref, o_ref, st_ref, *, n_heads):
    c = CHUNK
    n_batch, rows, _ = q_ref.shape
    items = [(ci, b) for ci in range(rows // c) for b in range(n_batch)]
    rsl = lambda ci: slice(ci * c, (ci + 1) * c)

    @pl.when(pl.program_id(0) == 0)
    def _():
        st_ref[...] = jnp.zeros_like(st_ref)

    lb = lb_ref[...]
    lb_floor = jnp.maximum(lb, LB_FLOOR)
    code = code_ref[...]
    seg = seg_ref[...]
    g = g_ref[...]

    ks, d_alls = {}, {}
    for ci, b in items:
        sig, sig_neg = _sigmoid_pair(f_ref[b, rsl(ci), :])
        log_f = jnp.log(lb_floor + (1.0 - lb) * sig)
        ks[ci, b] = (1.0 - lb) * sig_neg
        d_alls[ci, b] = _sel_dot(seg, -log_f)

    sls = [slice(h * HG_DK, (h + 1) * HG_DK) for h in range(n_heads)]

    def scores(ci, b):
        d_all = d_alls[ci, b]
        nb = d_all[SMALL_LEVELS * c:(SMALL_LEVELS + 1) * c, :]
        e_last = _exp_neg(nb[c - 1:c, :])
        e_b = _exp_neg(nb).astype(BF16)
        e_suf = _exp_neg(nb[c - 1:c, :] - nb).astype(BF16)
        e_lvl = [_exp_neg(d_all[l * c:(l + 1) * c, :]).astype(BF16) for l in range(SMALL_LEVELS)]
        for l in range(SMALL_LEVELS, LEVELS):
            m = 1 << l
            ref = jnp.concatenate(
                [jnp.broadcast_to(nb[s + m - 1:s + m, :], (2 * m, nb.shape[1])) for s in range(0, c, 2 * m)], axis=0)
            e_lvl.append(_exp_neg(jnp.abs(nb - ref)).astype(BF16))
        q, k, v = q_ref[b, rsl(ci), :].astype(BF16), ks[ci, b].astype(BF16), i_ref[b, rsl(ci), :].astype(BF16)
        out = []
        for sl in sls:
            qh, kh = q[:, sl], k[:, sl]
            sc = jnp.where(code == -1, _dot(qh, kh, NT), 0.0)
            for l in range(LEVELS):
                e = e_lvl[l][:, sl]
                sc = jnp.where(code == l, _dot(qh * e, kh * e, NT), sc)
            out.append(dict(sc=sc, v=v[:, sl], q_in=qh * e_b[:, sl], k_out=kh * e_suf[:, sl], e_last=e_last[:, sl]))
        return out

    def finish(ci, b, heads):
        for h, (sl, d) in enumerate(zip(sls, heads)):
            st = st_ref[b, h]
            o = _dot(d["sc"], d["v"]) + _dot(d["q_in"], st, NT)
            st_ref[b, h] = st * d["e_last"] + _dot(d["v"], d["k_out"], TN)
            o = o * lax.rsqrt(jnp.mean(o * o, axis=-1, keepdims=True) + RMS_EPS)
            o_ref[b, rsl(ci), sl] = (o * g[:, sl] * _silu(z_ref[b, rsl(ci), sl].astype(F32))).astype(o_ref.dtype)

    pending = scores(*items[0])
    for prev, cur in zip(items[:-1], items[1:]):
        nxt = scores(*cur)
        finish(*prev, pending)
        pending = nxt
    finish(*items[-1], pending)


def _hgrn_call(pf3, pb3, lb, norm_g, seg, code):
    batch, seq, _ = pf3.shape
    w = lb.shape[-1]
    n_heads = w // HG_DK
    col = lambda j: (lambda t: (0, t, j))
    const = lambda t: (0, 0)
    blk = (batch, STEP_CHUNKS * CHUNK, w)
    return pl.pallas_call(
        functools.partial(_hgrn_kernel, n_heads=n_heads),
        grid=(seq // (STEP_CHUNKS * CHUNK),),
        in_specs=[
            pl.BlockSpec(blk, col(0)),
            pl.BlockSpec(blk, col(1)),
            pl.BlockSpec(blk, col(0)),
            pl.BlockSpec(blk, col(1)),
            pl.BlockSpec((1, w), const),
            pl.BlockSpec((1, w), const),
            pl.BlockSpec(seg.shape, const),
            pl.BlockSpec(code.shape, const),
        ],
        out_specs=pl.BlockSpec(blk, col(0)),
        out_shape=jax.ShapeDtypeStruct((batch, seq, w), BF16),
        scratch_shapes=[pltpu.VMEM((batch, n_heads, HG_DK, HG_DK), F32)],
        compiler_params=pltpu.CompilerParams(
            dimension_semantics=("arbitrary",), vmem_limit_bytes=VMEM_LIMIT),
        name="hgrn2_branch",
    )(pf3, pf3, pb3, pb3, lb, norm_g, seg, code)


def _rwkv_kernel(*refs, first_layer):
    (r_ref, k_ref, v_ref, z_ref, lo_ref, mu_ref, mulo_ref, par_ref, wup_ref, aup_ref, vup_ref,
     bd_ref, tri_ref, code_ref) = refs[:14]
    if first_layer:
        vfirst_ref, (o_ref, vout_ref) = None, refs[14:16]
    else:
        vfirst_ref, o_ref, vout_ref = refs[14], refs[15], None
    prev_ref, prevlo_ref, st_ref = refs[16:]
    c = CHUNK
    n_batch, rows, w = r_ref.shape
    n_chunks = rows // c
    n_pairs = w // LANES

    @pl.when(pl.program_id(0) == 0)
    def _():
        prev_ref[...] = jnp.zeros_like(prev_ref)
        prevlo_ref[...] = jnp.zeros_like(prevlo_ref)
        st_ref[...] = jnp.zeros_like(st_ref)

    row0 = lax.broadcasted_iota(jnp.int32, (c, 1), 0) == 0
    bd = bd_ref[...]
    code = code_ref[...]
    tri = tri_ref[...]
    lane = lax.broadcasted_iota(jnp.int32, (1, LANES), 1)
    m_a = lane < RW_HEAD
    row_i = lax.broadcasted_iota(jnp.int32, (LANES, LANES), 0)
    col_i = lax.broadcasted_iota(jnp.int32, (LANES, LANES), 1)
    same_head = (row_i < RW_HEAD) == (col_i < RW_HEAD)
    strict = code >= 0
    incl = code >= -1
    eye = jnp.where(code == -1, 1.0, 0.0)
    sls = [slice(p * LANES, (p + 1) * LANES) for p in range(n_pairs)]

    def head_sum(x):
        return _dot(x, bd)

    def shifted(p, prev_row, m):
        prev = jnp.where(row0, prev_row, pltpu.roll(p, 1, 0))
        return p + (prev - p) * m

    def load_shifted(ref, ci, b, sl, carry_ref, carry_row, m):
        rs = slice(ci * c, (ci + 1) * c)
        cur = ref[b, rs, sl].astype(F32)
        if ci == 0:
            prev_row = carry_ref[b, carry_row:carry_row + 1, sl]
        else:
            prev_row = ref[b, ci * c - 1:ci * c, sl].astype(F32)
        if ci == n_chunks - 1:
            carry_ref[b, carry_row:carry_row + 1, sl] = cur[c - 1:c, :]
        return shifted(cur, prev_row, m)

    lo_cache = {}

    def low_rank_inputs(ci, b):
        if (ci, b) not in lo_cache:
            lo = load_shifted(lo_ref, ci, b, slice(None), prevlo_ref, 0, mulo_ref[...])
            lo_cache[ci, b] = (jnp.tanh(lo).astype(BF16), lo.astype(BF16))
        return lo_cache[ci, b]

    def prep(ci, b, p, pre):
        sl = sls[p]
        w0, a0, k_k, k_a, r_k, _, _, v0 = (par_ref[i:i + 1, sl] for i in range(8))
        rs = slice(ci * c, (ci + 1) * c)
        tanh_lo, lo = low_rank_inputs(ci, b)
        r, k, v, z = (load_shifted(ref, ci, b, sl, prev_ref, i, mu_ref[i:i + 1, sl])
                      for i, ref in enumerate((r_ref, k_ref, v_ref, z_ref)))
        gate = _silu(z)
        kk = k * k_k
        kk_sq = (kk * kk).astype(BF16)
        yield
        w_lin = lax.dot_general(tanh_lo, wup_ref[:, sl], NN, preferred_element_type=F32)
        a_lin = lax.dot_general(lo, aup_ref[:, sl], NN, preferred_element_type=F32)
        if not first_layer:
            v_lin = lax.dot_general(lo, vup_ref[:, sl], NN, preferred_element_type=F32)
        kk_ss = lax.dot_general(kk_sq, bd, NN, preferred_element_type=F32)
        yield
        lw = -math.exp(-0.5) * _sigmoid(w0 + w_lin)
        a = _sigmoid(a0 + a_lin)
        if first_layer:
            vout_ref[b, rs, sl] = v
        else:
            v = v + (vfirst_ref[b, rs, sl] - v) * _sigmoid(v0 + v_lin)
        kk = kk / jnp.maximum(jnp.sqrt(kk_ss), L2_EPS)
        k = k * (1.0 + (a - 1.0) * k_a)
        be = kk * a
        rkr = (r * k * r_k).astype(BF16)
        lw_parts = jnp.concatenate(_split2(lw), axis=-1)
        yield
        bonus_sum = lax.dot_general(rkr, bd, NN, preferred_element_type=F32)
        cum2 = lax.dot_general(tri, lw_parts, NN, preferred_element_type=F32)
        yield
        bonus = bonus_sum * v
        cum = cum2[:, 0:LANES] + cum2[:, LANES:2 * LANES]
        cum_prev = cum - lw
        c_mid = cum[c // 2 - 1:c // 2, :]
        c_end = cum[c - 1:c, :]
        e_mid_bwd = jnp.exp(c_mid - cum)
        e_end = jnp.exp(c_end - cum)
        pre[ci, b, p] = dict(
            r_abs=(r * jnp.exp(cum)).astype(BF16), kn_abs=(kk * jnp.exp(cum_prev)).astype(BF16),
            r_mid=(r * jnp.exp(cum - c_mid)).astype(BF16), kn_mid=(kk * jnp.exp(cum_prev - c_mid)).astype(BF16),
            k_mid=(k * e_mid_bwd).astype(BF16), b_mid=(be * e_mid_bwd).astype(BF16),
            k_end=(k * e_end).astype(BF16), b_end=(be * e_end).astype(BF16),
            w_end=jnp.exp(c_end), v=v.astype(BF16), bonus=bonus, gate=gate)

    def recur(pairs, pre, ys):
        n = len(pairs)
        heads = [(i, hh) for i in range(n) for hh in range(2)]
        ch = [pre[bp] for bp in pairs]
        gs = []
        for d in ch:
            zero = jnp.zeros_like(d["kn_mid"])
            lhs = jnp.concatenate([jnp.where(m_a, d["kn_mid"], zero), jnp.where(m_a, zero, d["kn_mid"]),
                                   jnp.where(m_a, d["r_mid"], zero), jnp.where(m_a, zero, d["r_mid"])], axis=0)
            rhs = jnp.concatenate([d["k_mid"], d["b_mid"]], axis=0)
            gs.append(_dot(lhs, rhs, NT))
        yield
        sts = [st_ref[b, p] for _, b, p in pairs]
        lows = [jnp.where(strict, gs[i][hh * c:(hh + 1) * c, c:2 * c], 0.0) for i, hh in heads]
        a1v = [_dot(jnp.where(strict, gs[i][hh * c:(hh + 1) * c, 0:c], 0.0), ch[i]["v"]) for i, hh in heads]
        x1 = [_dot(ch[i]["kn_abs"], sts[i], NT) + jnp.where(m_a, a1v[2 * i], a1v[2 * i + 1]) for i in range(n)]
        yield
        ts = [eye - jnp.where(code == 0, lo_h, 0.0) for lo_h in lows]
        for l in range(1, LEVELS):
            m = 1 << l
            offs = [jnp.where(code == l, lo_h, 0.0) for lo_h in lows]
            if m < SUBLANES:
                us = [_dot(t, off) for t, off in zip(ts, offs)]
                yield
                ts = [t - _dot(u, t) for t, u in zip(ts, us)]
                yield
            else:
                lower = [jnp.concatenate([t[s + m:s + 2 * m, :] for s in range(0, c, 2 * m)], axis=0) for t in ts]
                us = [_dot(lo_rows, off) for lo_rows, off in zip(lower, offs)]
                yield
                new = [lo_rows - _dot(u, t) for lo_rows, u, t in zip(lower, us, ts)]
                ts = [jnp.concatenate(
                    [piece for j, s in enumerate(range(0, c, 2 * m))
                     for piece in (t[s:s + m, :], nw[j * m:(j + 1) * m, :])], axis=0) for t, nw in zip(ts, new)]
                yield
        sa = [(-jnp.where(m_a, _dot(ts[2 * i], x1[i]), _dot(ts[2 * i + 1], x1[i]))).astype(BF16) for i in range(n)]
        vs = [jnp.concatenate([ch[i]["v"], sa[i]], axis=0) for i in range(n)]
        yield
        a34 = [jnp.concatenate([jnp.where(incl, gs[i][(2 + hh) * c:(3 + hh) * c, 0:c], 0.0),
                                jnp.where(incl, gs[i][(2 + hh) * c:(3 + hh) * c, c:2 * c], 0.0)], axis=1)
               for i, hh in heads]
        yh = [_dot(a34[j], vs[j // 2]) for j in range(len(heads))]
        yield
        for i, (ci, b, p) in enumerate(pairs):
            zed = jnp.concatenate([ch[i]["k_end"], ch[i]["b_end"]], axis=0)
            st_ref[b, p] = sts[i] * ch[i]["w_end"] + jnp.where(same_head, _dot(vs[i], zed, TN), 0.0)
            ys[ci, b, p] = _dot(ch[i]["r_abs"], sts[i], NT) + jnp.where(m_a, yh[2 * i], yh[2 * i + 1])
        yield

    def finish(ci, b, p, y, d):
        sl = sls[p]
        inv_n = 1.0 / RW_HEAD
        y_sum = head_sum(y)
        yield
        yc = y - y_sum * inv_n
        yc_sq = (yc * yc).astype(BF16)
        yield
        var = lax.dot_general(yc_sq, bd, NN, preferred_element_type=F32) * inv_n
        yield
        yn = yc * lax.rsqrt(var + GN_EPS) * par_ref[5:6, sl] + par_ref[6:7, sl]
        o_ref[b, ci * c:(ci + 1) * c, sl] = ((yn + d["bonus"]) * d["gate"]).astype(o_ref.dtype)

    def run_all(gens):
        gens = list(gens)
        while gens:
            gens = [g for g in gens if next(g, StopIteration) is not StopIteration]

    group = 2 if n_batch % 2 == 0 else 1
    groups = [[(ci, b, p) for b in range(g0, g0 + group) for p in range(n_pairs)]
              for ci in range(n_chunks) for g0 in range(0, n_batch, group)]
    pre, ys = {}, {}
    run_all(prep(*bp, pre) for bp in groups[0])
    for gi, pairs in enumerate(groups):
        fill = [prep(*bp, pre) for bp in (groups[gi + 1] if gi + 1 < len(groups) else [])]
        fill += [finish(*bp, ys[bp], pre[bp]) for bp in (groups[gi - 1] if gi > 0 else [])]
        n_stages = 2 * LEVELS + 2
        start = [(j * (n_stages - 4)) // max(len(fill), 1) for j in range(len(fill))]
        active = []
        for s, _ in enumerate(recur(pairs, pre, ys)):
            active += [g for j, g in enumerate(fill) if start[j] == s]
            active = [g for g in active if next(g, StopIteration) is not StopIteration]
        run_all(active)
    run_all(finish(*bp, ys[bp], pre[bp]) for bp in groups[-1])


def _rwkv_call(pf3, pb3, mu4, mu_lo, par, wup, aup, vup, bd, tri, code, v_first, col0, lo_col):
    batch, seq, _ = pf3.shape
    w = par.shape[-1]
    wl = mu_lo.shape[-1]
    first_layer = v_first is None
    col = lambda j: (lambda t: (0, t, j))
    const = lambda t: (0, 0)
    blk = (batch, STEP_CHUNKS * CHUNK, w)
    row_spec = pl.BlockSpec(blk, col(0))
    in_specs = [
        pl.BlockSpec(blk, col(col0)),
        pl.BlockSpec(blk, col(col0 + 1)),
        pl.BlockSpec(blk, col(col0 + 2)),
        pl.BlockSpec(blk, col(col0 + 3)),
        pl.BlockSpec((batch, STEP_CHUNKS * CHUNK, wl), col(lo_col)),
        pl.BlockSpec(mu4.shape, const),
        pl.BlockSpec(mu_lo.shape, const),
        pl.BlockSpec(par.shape, const),
        pl.BlockSpec(wup.shape, const),
        pl.BlockSpec(aup.shape, const),
        pl.BlockSpec(vup.shape, const),
        pl.BlockSpec(bd.shape, const),
        pl.BlockSpec(tri.shape, const),
        pl.BlockSpec(code.shape, const),
    ]
    args = [pb3, pb3, pb3, pb3, pf3, mu4, mu_lo, par, wup, aup, vup, bd, tri, code]
    out_sds = jax.ShapeDtypeStruct((batch, seq, w), BF16)
    if first_layer:
        out_specs = [row_spec, row_spec]
        out_shape = [out_sds, jax.ShapeDtypeStruct((batch, seq, w), F32)]
    else:
        in_specs.append(row_spec)
        args.append(v_first)
        out_specs = row_spec
        out_shape = out_sds
    return pl.pallas_call(
        functools.partial(_rwkv_kernel, first_layer=first_layer),
        grid=(seq // (STEP_CHUNKS * CHUNK),),
        in_specs=in_specs,
        out_specs=out_specs,
        out_shape=out_shape,
        scratch_shapes=[
            pltpu.VMEM((batch, SUBLANES, w), F32),
            pltpu.VMEM((batch, SUBLANES, wl), F32),
            pltpu.VMEM((batch, w // LANES, LANES, LANES), F32),
        ],
        compiler_params=pltpu.CompilerParams(
            dimension_semantics=("arbitrary",), vmem_limit_bytes=VMEM_LIMIT),
        name="rwkv7_branch_first" if first_layer else "rwkv7_branch",
    )(*args)


def _out_kernel(x_ref, ohg_ref, orw_ref, ghg_ref, grw_ref, gate_ref, wa_ref, wb_ref, wo_ref,
                lng_ref, lnb_ref, o_ref, *, alpha):
    d = functools.partial(lax.dot_general, dimension_numbers=NN, preferred_element_type=F32)
    y_hg = d(ohg_ref[...].astype(BF16), wa_ref[...])
    y_rw = d(orw_ref[...].astype(BF16), wb_ref[...])
    merged = _sigmoid(ghg_ref[...].astype(F32)) * y_hg + _sigmoid(grw_ref[...].astype(F32)) * y_rw
    out = d(merged.astype(BF16), wo_ref[...])
    xn = alpha * x_ref[...] + (1.0 + gate_ref[0]) * out
    mu = jnp.mean(xn, axis=-1, keepdims=True)
    xc = xn - mu
    var = jnp.mean(xc * xc, axis=-1, keepdims=True)
    o_ref[...] = xc * lax.rsqrt(var + LN_EPS) * lng_ref[...] + lnb_ref[...]


def _out_call(x2, o_hg, o_rw, proj, gate, wa, wb, wo, ln_g, ln_b, seq, tm, gate_col, alpha):
    bt, d = x2.shape
    wh = o_hg.shape[1]
    per_b = seq // tm
    row = lambda i: (i, 0)
    const = lambda i: (0, 0)
    return pl.pallas_call(
        functools.partial(_out_kernel, alpha=alpha),
        grid=(bt // tm,),
        in_specs=[
            pl.BlockSpec((tm, d), row),
            pl.BlockSpec((tm, wh), row),
            pl.BlockSpec((tm, wh), row),
            pl.BlockSpec((tm, d), lambda i: (i, gate_col)),
            pl.BlockSpec((tm, d), lambda i: (i, gate_col + 1)),
            pl.BlockSpec((1, 1, d), lambda i: (i // per_b, 0, 0)),
            pl.BlockSpec(wa.shape, const),
            pl.BlockSpec(wb.shape, const),
            pl.BlockSpec(wo.shape, const),
            pl.BlockSpec((1, d), const),
            pl.BlockSpec((1, d), const),
        ],
        out_specs=pl.BlockSpec((tm, d), row),
        out_shape=jax.ShapeDtypeStruct((bt, d), F32),
        compiler_params=pltpu.CompilerParams(
            dimension_semantics=("parallel",), vmem_limit_bytes=VMEM_LIMIT),
        name="merge_out_proj_norm",
    )(x2, o_hg, o_rw, proj, proj, gate, wa, wb, wo, ln_g, ln_b)


def kernel(x, c, w_ada, b_ada, w_in, hg_lower_bounds, hg_norm_g, rw_mu, rw_w0, rw_w_up, rw_a0, rw_a_up,
           rw_k_k, rw_k_a, rw_r_k, rw_v0, rw_v_down, rw_v_up, rw_gn_g, rw_gn_b, w_branch_hg, w_branch_rw,
           w_out, ln_g, ln_b):
    batch, seq, d = x.shape
    depth = w_in.shape[0]
    hgw = hg_norm_g.shape[1]
    rww = rw_w0.shape[1]
    r_decay = rw_w_up.shape[1]
    r_icl = rw_a_up.shape[1]
    r_vres = rw_v_up.shape[1]
    hg_cols = 4 * hgw
    rw_main = 4 * rww
    lo_w = 2 * LANES
    assert r_decay + r_icl + r_vres <= lo_w and hgw == rww and d == 2 * hgw
    assert seq % (STEP_CHUNKS * CHUNK) == 0
    alpha = float((2 * depth) ** 0.25)

    lb_soft = jax.nn.softmax(hg_lower_bounds.astype(F32), axis=0)
    lower_bounds = jnp.cumsum(lb_soft, axis=0) - lb_soft[0]

    seg = jnp.asarray(_hgrn_segment_matrix(CHUNK), BF16)
    code = jnp.asarray(_split_code(CHUNK), jnp.int32)
    tri = jnp.asarray(np.tril(np.ones((CHUNK, CHUNK), np.float32)), BF16)
    bd = jnp.asarray(_block_ones(LANES, RW_HEAD), BF16)

    c_pad = jnp.zeros((8, d), F32).at[:batch].set(c)
    cond = _ada_call(c_pad, w_ada, b_ada)

    x2 = x.reshape(batch * seq, d)
    tm = min(1024, seq)
    v_first = None
    for l in range(depth):
        shift = cond[l, :batch, 0:d].reshape(batch, 1, d)
        scale = cond[l, :batch, d:2 * d].reshape(batch, 1, d)
        gate = cond[l, :batch, 2 * d:3 * d].reshape(batch, 1, d)

        wl = w_in[l]
        w_q, w_f, w_i, w_z = (wl[:, j * hgw:(j + 1) * hgw] for j in range(4))
        w_rw = wl[:, hg_cols:hg_cols + rw_main]
        w_lo = wl[:, hg_cols + rw_main:hg_cols + rw_main + r_decay + r_icl]
        w_gate = wl[:, hg_cols + rw_main + r_decay + r_icl:]
        w_vd = rw_v_down[l - 1] if l > 0 else jnp.zeros((d, r_vres), F32)
        pad = jnp.zeros((d, lo_w - (r_decay + r_icl + r_vres)), F32)
        w_f32 = jnp.concatenate([w_q, w_f, w_lo, w_vd, pad], axis=1).astype(BF16)
        w_b16 = jnp.concatenate([w_i, w_z, w_gate, w_rw], axis=1).astype(BF16)
        assert w_b16.shape[1] % (2 * LANES) == 0
        proj_f, proj_b = _proj_call(x2, shift, scale, w_f32, w_b16, seq, tm, w_b16.shape[1] // 2)

        pf3 = proj_f.reshape(batch, seq, w_f32.shape[1])
        pb3 = proj_b.reshape(batch, seq, w_b16.shape[1])
        o_hg = _hgrn_call(pf3, pb3, lower_bounds[l].reshape(1, hgw), hg_norm_g[l].reshape(1, hgw), seg, code)

        mu_l = rw_mu[l]
        mu4 = jnp.zeros((8, rww), F32).at[0:4].set(mu_l[:rw_main].reshape(4, rww))
        mu_lo = jnp.zeros((1, lo_w), F32).at[0, :r_decay + r_icl].set(mu_l[rw_main:])
        v0 = rw_v0[l - 1] if l > 0 else jnp.zeros((rww,), F32)
        par = jnp.stack([rw_w0[l], rw_a0[l], rw_k_k[l], rw_k_a[l], rw_r_k[l].reshape(rww),
                         rw_gn_g[l], rw_gn_b[l], v0], axis=0)
        wup = jnp.zeros((lo_w, rww), F32).at[0:r_decay].set(rw_w_up[l]).astype(BF16)
        aup = jnp.zeros((lo_w, rww), F32).at[r_decay:r_decay + r_icl].set(rw_a_up[l]).astype(BF16)
        vup_l = rw_v_up[l - 1] if l > 0 else jnp.zeros((r_vres, rww), F32)
        vup = jnp.zeros((lo_w, rww), F32).at[r_decay + r_icl:r_decay + r_icl + r_vres].set(vup_l).astype(BF16)
        col0 = (2 * hgw + w_gate.shape[1]) // rww
        res = _rwkv_call(pf3, pb3, mu4, mu_lo, par, wup, aup, vup, bd, tri, code, v_first, col0, 2 * hgw // lo_w)
        if l == 0:
            o_rw, v_first = res
        else:
            o_rw = res

        x2 = _out_call(x2, o_hg.reshape(batch * seq, hgw), o_rw.reshape(batch * seq, rww), proj_b, gate,
                       w_branch_hg[l].astype(BF16), w_branch_rw[l].astype(BF16), w_out[l].astype(BF16),
                       ln_g[l].reshape(1, d), ln_b[l].reshape(1, d), seq, tm, 2 * hgw // d, alpha)
    return x2.reshape(batch, seq, d)
```

```python
import functools
import math

import numpy as np
import jax
import jax.numpy as jnp
from jax import lax
from jax.experimental import pallas as pl
from jax.experimental.pallas import tpu as pltpu

F32 = jnp.float32
BF16 = jnp.bfloat16

HG_DK = 128
RW_HEAD = 64
LN_EPS = 1e-5
RMS_EPS = 1e-6
GN_EPS = 64e-5
L2_EPS = 1e-12
LB_FLOOR = 1e-30

LANES = 128
SUBLANES = 8
CHUNK = 128
LEVELS = int(math.log2(CHUNK))
STEP_CHUNKS = 2
SMALL_LEVELS = int(math.log2(SUBLANES))
VMEM_LIMIT = 56 * 1024 * 1024

NT = (((1,), (1,)), ((), ()))
TN = (((0,), (0,)), ((), ()))
NN = (((1,), (0,)), ((), ()))


def _dot(a, b, dims=NN):
    return lax.dot_general(a.astype(BF16), b.astype(BF16), dims, preferred_element_type=F32)


def _split2(x):
    hi = x.astype(BF16)
    lo = (x - hi.astype(F32)).astype(BF16)
    return hi, lo


def _dot3(a, b):
    ah, al = _split2(a)
    bh, bl = _split2(b)
    d = functools.partial(lax.dot_general, dimension_numbers=NN, preferred_element_type=F32)
    return d(ah, bh) + (d(ah, bl) + d(al, bh))


def _sel_dot(m01_twice, x):
    return lax.dot_general(m01_twice, jnp.concatenate(_split2(x), axis=0), NN, preferred_element_type=F32)


def _sigmoid(x):
    return 1.0 / (1.0 + jnp.exp(-x))


def _silu(x):
    return x * _sigmoid(x)


def _sigmoid_pair(x):
    t = jnp.exp(-jnp.abs(x))
    big = 1.0 / (1.0 + t)
    small = t * big
    pos = x >= 0.0
    return jnp.where(pos, big, small), jnp.where(pos, small, big)


def _split_code(c):
    t = np.arange(c)[:, None]
    s = np.arange(c)[None, :]
    x = t ^ s
    code = np.where(x > 0, np.floor(np.log2(np.maximum(x, 1))).astype(np.int32), -1)
    code = np.where(s > t, -2, code)
    return code.astype(np.int32)


def _hgrn_segment_matrix(c):
    blocks = []
    j = np.arange(c)[None, :]
    t = np.arange(c)[:, None]
    for l in range(SMALL_LEVELS):
        m = 1 << l
        start = (t // (2 * m)) * (2 * m)
        mid = start + m - 1
        second = ((t // m) % 2) == 1
        blk = np.where(second, (j > mid) & (j <= t), (j > t) & (j <= mid))
        blocks.append(blk)
    blocks.append(j <= t)
    return np.concatenate(blocks, axis=0).astype(np.float32)


def _block_ones(n, blk):
    i = np.arange(n)
    return (i[:, None] // blk == i[None, :] // blk).astype(np.float32)


def _ada_kernel(c_ref, w_ref, b_ref, o_ref):
    sc = _silu(c_ref[...])
    o_ref[0] = _dot3(sc, w_ref[0]) + b_ref[0]


def _ada_call(c_pad, w_ada, b_ada):
    depth, d, d3 = w_ada.shape
    rows = c_pad.shape[0]
    tn = d
    return pl.pallas_call(
        _ada_kernel,
        grid=(depth, d3 // tn),
        in_specs=[
            pl.BlockSpec((rows, d), lambda l, j: (0, 0)),
            pl.BlockSpec((1, d, tn), lambda l, j: (l, 0, j)),
            pl.BlockSpec((1, 1, tn), lambda l, j: (l, 0, j)),
        ],
        out_specs=pl.BlockSpec((1, rows, tn), lambda l, j: (l, 0, j)),
        out_shape=jax.ShapeDtypeStruct((depth, rows, d3), F32),
        compiler_params=pltpu.CompilerParams(
            dimension_semantics=("parallel", "parallel"), vmem_limit_bytes=VMEM_LIMIT),
        name="adaln_cond",
    )(c_pad, w_ada, b_ada.reshape(depth, 1, d3))


def _proj_kernel(x_ref, shift_ref, scale_ref, wf_ref, wb_ref, of_ref, ob_ref, h_ref):
    j = pl.program_id(1)

    @pl.when(j == 0)
    def _():
        h = x_ref[...] * (1.0 + scale_ref[0]) + shift_ref[0]
        h_ref[...] = h.astype(BF16)
        of_ref[...] = lax.dot_general(h_ref[...], wf_ref[...], NN, preferred_element_type=F32)

    @pl.when(j > 0)
    def _():
        ob_ref[...] = lax.dot_general(h_ref[...], wb_ref[...], NN, preferred_element_type=F32).astype(BF16)


def _proj_call(x2, shift, scale, w_f, w_b, seq, tm, tn_b):
    bt, d = x2.shape
    n_f, n_b = w_f.shape[1], w_b.shape[1]
    per_b = seq // tm
    b_tile = lambda i, j: (i, jnp.maximum(j - 1, 0))
    return pl.pallas_call(
        _proj_kernel,
        grid=(bt // tm, 1 + n_b // tn_b),
        in_specs=[
            pl.BlockSpec((tm, d), lambda i, j: (i, 0)),
            pl.BlockSpec((1, 1, d), lambda i, j: (i // per_b, 0, 0)),
            pl.BlockSpec((1, 1, d), lambda i, j: (i // per_b, 0, 0)),
            pl.BlockSpec((d, n_f), lambda i, j: (0, 0)),
            pl.BlockSpec((d, tn_b), lambda i, j: (0, jnp.maximum(j - 1, 0))),
        ],
        out_specs=[pl.BlockSpec((tm, n_f), lambda i, j: (i, 0)), pl.BlockSpec((tm, tn_b), b_tile)],
        out_shape=[jax.ShapeDtypeStruct((bt, n_f), F32), jax.ShapeDtypeStruct((bt, n_b), BF16)],
        scratch_shapes=[pltpu.VMEM((tm, d), BF16)],
        compiler_params=pltpu.CompilerParams(
            dimension_semantics=("parallel", "arbitrary"), vmem_limit_bytes=VMEM_LIMIT),
        name="modulate_in_proj",
    )(x2, shift, scale, w_f, w_b)


def _hgrn_kernel(q_ref, f_ref, i_ref, z_ref, lb_ref, g_ref, seg_ref, code_ref, o_ref, st_ref, *, n_heads):
    c = CHUNK
    n_batch, rows, _ = q_ref.shape
    items = [(ci, b) for ci in range(rows // c) for b in range(n_batch)]
    rsl = lambda ci: slice(ci * c, (ci + 1) * c)

    @pl.when(pl.program_id(0) == 0)
    def _():
        st_ref[...] = jnp.zeros_like(st_ref)

    lb = lb_ref[...]
    lb_floor = jnp.maximum(lb, LB_FLOOR)
    code = code_ref[...]
    seg = seg_ref[...]
    g = g_ref[...]

    ks, d_alls = {}, {}
    for ci, b in items:
        sig, sig_neg = _sigmoid_pair(f_ref[b, rsl(ci), :])
        log2_f = jnp.log(lb_floor + (1.0 - lb) * sig) * math.log2(math.e)
        ks[ci, b] = (1.0 - lb) * sig_neg
        d_alls[ci, b] = _sel_dot(seg, log2_f)

    sls = [slice(h * HG_DK, (h + 1) * HG_DK) for h in range(n_heads)]

    def scores(ci, b):
        d_all = d_alls[ci, b]
        nb = d_all[SMALL_LEVELS * c:(SMALL_LEVELS + 1) * c, :]
        e_last = jnp.exp2(nb[c - 1:c, :])
        e_b = jnp.exp2(nb).astype(BF16)
        e_suf = jnp.exp2(nb[c - 1:c, :] - nb).astype(BF16)
        e_lvl = [jnp.exp2(d_all[l * c:(l + 1) * c, :]).astype(BF16) for l in range(SMALL_LEVELS)]
        for l in range(SMALL_LEVELS, LEVELS):
            m = 1 << l
            ref = jnp.concatenate(
                [jnp.broadcast_to(nb[s + m - 1:s + m, :], (2 * m, nb.shape[1])) for s in range(0, c, 2 * m)], axis=0)
            e_lvl.append(jnp.exp2(-jnp.abs(nb - ref)).astype(BF16))
        q, k, v = q_ref[b, rsl(ci), :].astype(BF16), ks[ci, b].astype(BF16), i_ref[b, rsl(ci), :].astype(BF16)
        out = []
        for sl in sls:
            qh, kh = q[:, sl], k[:, sl]
            sc = jnp.where(code == -1, _dot(qh, kh, NT), 0.0)
            for l in range(LEVELS):
                e = e_lvl[l][:, sl]
                sc = jnp.where(code == l, _dot(qh * e, kh * e, NT), sc)
            out.append(dict(sc=sc, v=v[:, sl], q_in=qh * e_b[:, sl], k_out=kh * e_suf[:, sl], e_last=e_last[:, sl]))
        return out

    def finish(ci, b, heads):
        for h, (sl, d) in enumerate(zip(sls, heads)):
            st = st_ref[b, h]
            o = _dot(d["sc"], d["v"]) + _dot(d["q_in"], st, NT)
            st_ref[b, h] = st * d["e_last"] + _dot(d["v"], d["k_out"], TN)
            o = o * lax.rsqrt(jnp.mean(o * o, axis=-1, keepdims=True) + RMS_EPS)
            o_ref[b, rsl(ci), sl] = (o * g[:, sl] * _silu(z_ref[b, rsl(ci), sl].astype(F32))).astype(o_ref.dtype)

    pending = scores(*items[0])
    for prev, cur in zip(items[:-1], items[1:]):
        nxt = scores(*cur)
        finish(*prev, pending)
        pending = nxt
    finish(*items[-1], pending)


def _hgrn_call(pf3, pb3, lb, norm_g, seg, code):
    batch, seq, _ = pf3.shape
    w = lb.shape[-1]
    n_heads = w // HG_DK
    col = lambda j: (lambda t: (0, t, j))
    const = lambda t: (0, 0)
    blk = (batch, STEP_CHUNKS * CHUNK, w)
    return pl.pallas_call(
        functools.partial(_hgrn_kernel, n_heads=n_heads),
        grid=(seq // (STEP_CHUNKS * CHUNK),),
        in_specs=[
            pl.BlockSpec(blk, col(0)),
            pl.BlockSpec(blk, col(1)),
            pl.BlockSpec(blk, col(0)),
            pl.BlockSpec(blk, col(1)),
            pl.BlockSpec((1, w), const),
            pl.BlockSpec((1, w), const),
            pl.BlockSpec(seg.shape, const),
            pl.BlockSpec(code.shape, const),
        ],
        out_specs=pl.BlockSpec(blk, col(0)),
        out_shape=jax.ShapeDtypeStruct((batch, seq, w), BF16),
        scratch_shapes=[pltpu.VMEM((batch, n_heads, HG_DK, HG_DK), F32)],
        compiler_params=pltpu.CompilerParams(
            dimension_semantics=("arbitrary",), vmem_limit_bytes=VMEM_LIMIT),
        name="hgrn2_branch",
    )(pf3, pf3, pb3, pb3, lb, norm_g, seg, code)


def _rwkv_kernel(*refs, first_layer):
    (r_ref, k_ref, v_ref, z_ref, lo_ref, mu_ref, mulo_ref, par_ref, wup_ref, aup_ref, vup_ref,
     bd_ref, tri_ref, code_ref) = refs[:14]
    if first_layer:
        vfirst_ref, (o_ref, vout_ref) = None, refs[14:16]
    else:
        vfirst_ref, o_ref, vout_ref = refs[14], refs[15], None
    prev_ref, prevlo_ref, st_ref = refs[16:]
    c = CHUNK
    n_batch, rows, w = r_ref.shape
    n_chunks = rows // c
    n_pairs = w // LANES

    @pl.when(pl.program_id(0) == 0)
    def _():
        prev_ref[...] = jnp.zeros_like(prev_ref)
        prevlo_ref[...] = jnp.zeros_like(prevlo_ref)
        st_ref[...] = jnp.zeros_like(st_ref)

    row0 = lax.broadcasted_iota(jnp.int32, (c, 1), 0) == 0
    bd = bd_ref[...]
    code = code_ref[...]
    tri = tri_ref[...]
    lane = lax.broadcasted_iota(jnp.int32, (1, LANES), 1)
    m_a = lane < RW_HEAD
    row_i = lax.broadcasted_iota(jnp.int32, (LANES, LANES), 0)
    col_i = lax.broadcasted_iota(jnp.int32, (LANES, LANES), 1)
    same_head = (row_i < RW_HEAD) == (col_i < RW_HEAD)
    strict = code >= 0
    incl = code >= -1
    eye = jnp.where(code == -1, 1.0, 0.0)
    sls = [slice(p * LANES, (p + 1) * LANES) for p in range(n_pairs)]

    def head_sum(x):
        return _dot(x, bd)

    def shifted(p, prev_row, m):
        prev = jnp.where(row0, prev_row, pltpu.roll(p, 1, 0))
        return p + (prev - p) * m

    def load_shifted(ref, ci, b, sl, carry_ref, carry_row, m):
        rs = slice(ci * c, (ci + 1) * c)
        cur = ref[b, rs, sl].astype(F32)
        if ci == 0:
            prev_row = carry_ref[b, carry_row:carry_row + 1, sl]
        else:
            prev_row = ref[b, ci * c - 1:ci * c, sl].astype(F32)
        if ci == n_chunks - 1:
            carry_ref[b, carry_row:carry_row + 1, sl] = cur[c - 1:c, :]
        return shifted(cur, prev_row, m)

    lo_cache = {}

    def low_rank_inputs(ci, b):
        if (ci, b) not in lo_cache:
            lo = load_shifted(lo_ref, ci, b, slice(None), prevlo_ref, 0, mulo_ref[...])
            lo_cache[ci, b] = (jnp.tanh(lo).astype(BF16), lo.astype(BF16))
        return lo_cache[ci, b]

    def prep(ci, b, p, pre):
        sl = sls[p]
        w0, a0, k_k, k_a, r_k, _, _, v0 = (par_ref[i:i + 1, sl] for i in range(8))
        rs = slice(ci * c, (ci + 1) * c)
        tanh_lo, lo = low_rank_inputs(ci, b)
        r, k, v, z = (load_shifted(ref, ci, b, sl, prev_ref, i, mu_ref[i:i + 1, sl])
                      for i, ref in enumerate((r_ref, k_ref, v_ref, z_ref)))
        gate = _silu(z)
        kk = k * k_k
        kk_sq = (kk * kk).astype(BF16)
        yield
        w_lin = lax.dot_general(tanh_lo, wup_ref[:, sl], NN, preferred_element_type=F32)
        a_lin = lax.dot_general(lo, aup_ref[:, sl], NN, preferred_element_type=F32)
        if not first_layer:
            v_lin = lax.dot_general(lo, vup_ref[:, sl], NN, preferred_element_type=F32)
        kk_ss = lax.dot_general(kk_sq, bd, NN, preferred_element_type=F32)
        yield
        lw = -math.exp(-0.5) * _sigmoid(w0 + w_lin)
        a = _sigmoid(a0 + a_lin)
        if first_layer:
            vout_ref[b, rs, sl] = v
        else:
            v = v + (vfirst_ref[b, rs, sl] - v) * _sigmoid(v0 + v_lin)
        kk = kk / jnp.maximum(jnp.sqrt(kk_ss), L2_EPS)
        k = k * (1.0 + (a - 1.0) * k_a)
        be = kk * a
        rkr = (r * k * r_k).astype(BF16)
        yield
        bonus_sum = lax.dot_general(rkr, bd, NN, preferred_element_type=F32)
        cum = _sel_dot(tri, lw)
        yield
        bonus = bonus_sum * v
        cum_prev = cum - lw
        c_mid = cum[c // 2 - 1:c // 2, :]
        c_end = cum[c - 1:c, :]
        e_mid_bwd = jnp.exp(c_mid - cum)
        e_end = jnp.exp(c_end - cum)
        pre[ci, b, p] = dict(
            r_abs=(r * jnp.exp(cum)).astype(BF16), kn_abs=(kk * jnp.exp(cum_prev)).astype(BF16),
            r_mid=(r * jnp.exp(cum - c_mid)).astype(BF16), kn_mid=(kk * jnp.exp(cum_prev - c_mid)).astype(BF16),
            k_mid=(k * e_mid_bwd).astype(BF16), b_mid=(be * e_mid_bwd).astype(BF16),
            k_end=(k * e_end).astype(BF16), b_end=(be * e_end).astype(BF16),
            w_end=jnp.exp(c_end), v=v.astype(BF16), bonus=bonus, gate=gate)

    def recur(pairs, pre, ys):
        n = len(pairs)
        heads = [(i, hh) for i in range(n) for hh in range(2)]
        ch = [pre[bp] for bp in pairs]
        gs = []
        for d in ch:
            zero = jnp.zeros_like(d["kn_mid"])
            lhs = jnp.concatenate([jnp.where(m_a, d["kn_mid"], zero), jnp.where(m_a, zero, d["kn_mid"]),
                                   jnp.where(m_a, d["r_mid"], zero), jnp.where(m_a, zero, d["r_mid"])], axis=0)
            rhs = jnp.concatenate([d["k_mid"], d["b_mid"]], axis=0)
            gs.append(_dot(lhs, rhs, NT))
        yield
        sts = [st_ref[b, p] for _, b, p in pairs]
        lows = [jnp.where(strict, gs[i][hh * c:(hh + 1) * c, c:2 * c], 0.0) for i, hh in heads]
        a1v = [_dot(jnp.where(strict, gs[i][hh * c:(hh + 1) * c, 0:c], 0.0), ch[i]["v"]) for i, hh in heads]
        x1 = [_dot(ch[i]["kn_abs"], sts[i], NT) + jnp.where(m_a, a1v[2 * i], a1v[2 * i + 1]) for i in range(n)]
        yield
        ts = [eye - jnp.where(code == 0, lo_h, 0.0) for lo_h in lows]
        for l in range(1, LEVELS):
            m = 1 << l
            offs = [jnp.where(code == l, lo_h, 0.0) for lo_h in lows]
            if m < SUBLANES:
                us = [_dot(t, off) for t, off in zip(ts, offs)]
                yield
                ts = [t - _dot(u, t) for t, u in zip(ts, us)]
                yield
            else:
                lower = [jnp.concatenate([t[s + m:s + 2 * m, :] for s in range(0, c, 2 * m)], axis=0) for t in ts]
                us = [_dot(lo_rows, off) for lo_rows, off in zip(lower, offs)]
                yield
                new = [lo_rows - _dot(u, t) for lo_rows, u, t in zip(lower, us, ts)]
                ts = [jnp.concatenate(
                    [piece for j, s in enumerate(range(0, c, 2 * m))
                     for piece in (t[s:s + m, :], nw[j * m:(j + 1) * m, :])], axis=0) for t, nw in zip(ts, new)]
                yield
        sa = [(-jnp.where(m_a, _dot(ts[2 * i], x1[i]), _dot(ts[2 * i + 1], x1[i]))).astype(BF16) for i in range(n)]
        vs = [jnp.concatenate([ch[i]["v"], sa[i]], axis=0) for i in range(n)]
        yield
        a34 = [jnp.concatenate([jnp.where(incl, gs[i][(2 + hh) * c:(3 + hh) * c, 0:c], 0.0),
                                jnp.where(incl, gs[i][(2 + hh) * c:(3 + hh) * c, c:2 * c], 0.0)], axis=1)
               for i, hh in heads]
        yh = [_dot(a34[j], vs[j // 2]) for j in range(len(heads))]
        yield
        for i, (ci, b, p) in enumerate(pairs):
            zed = jnp.concatenate([ch[i]["k_end"], ch[i]["b_end"]], axis=0)
            st_ref[b, p] = sts[i] * ch[i]["w_end"] + jnp.where(same_head, _dot(vs[i], zed, TN), 0.0)
            ys[ci, b, p] = _dot(ch[i]["r_abs"], sts[i], NT) + jnp.where(m_a, yh[2 * i], yh[2 * i + 1])
        yield

    def finish(ci, b, p, y, d):
        sl = sls[p]
        inv_n = 1.0 / RW_HEAD
        y_sum = head_sum(y)
        yield
        yc = y - y_sum * inv_n
        yc_sq = (yc * yc).astype(BF16)
        yield
        var = lax.dot_general(yc_sq, bd, NN, preferred_element_type=F32) * inv_n
        yield
        yn = yc * lax.rsqrt(var + GN_EPS) * par_ref[5:6, sl] + par_ref[6:7, sl]
        o_ref[b, ci * c:(ci + 1) * c, sl] = ((yn + d["bonus"]) * d["gate"]).astype(o_ref.dtype)

    def run_all(gens):
        gens = list(gens)
        while gens:
            gens = [g for g in gens if next(g, StopIteration) is not StopIteration]

    group = 2 if n_batch % 2 == 0 else 1
    groups = [[(ci, b, p) for b in range(g0, g0 + group) for p in range(n_pairs)]
              for ci in range(n_chunks) for g0 in range(0, n_batch, group)]
    pre, ys = {}, {}
    run_all(prep(*bp, pre) for bp in groups[0])
    for gi, pairs in enumerate(groups):
        fill = [prep(*bp, pre) for bp in (groups[gi + 1] if gi + 1 < len(groups) else [])]
        fill += [finish(*bp, ys[bp], pre[bp]) for bp in (groups[gi - 1] if gi > 0 else [])]
        n_stages = 2 * LEVELS + 2
        start = [(j * (n_stages - 4)) // max(len(fill), 1) for j in range(len(fill))]
        active = []
        for s, _ in enumerate(recur(pairs, pre, ys)):
            active += [g for j, g in enumerate(fill) if start[j] == s]
            active = [g for g in active if next(g, StopIteration) is not StopIteration]
        run_all(active)
    run_all(finish(*bp, ys[bp], pre[bp]) for bp in groups[-1])


def _rwkv_call(pf3, pb3, mu4, mu_lo, par, wup, aup, vup, bd, tri, code, v_first, col0, lo_col):
    batch, seq, _ = pf3.shape
    w = par.shape[-1]
    wl = mu_lo.shape[-1]
    first_layer = v_first is None
    col = lambda j: (lambda t: (0, t, j))
    const = lambda t: (0, 0)
    blk = (batch, STEP_CHUNKS * CHUNK, w)
    row_spec = pl.BlockSpec(blk, col(0))
    in_specs = [
        pl.BlockSpec(blk, col(col0)),
        pl.BlockSpec(blk, col(col0 + 1)),
        pl.BlockSpec(blk, col(col0 + 2)),
        pl.BlockSpec(blk, col(col0 + 3)),
        pl.BlockSpec((batch, STEP_CHUNKS * CHUNK, wl), col(lo_col)),
        pl.BlockSpec(mu4.shape, const),
        pl.BlockSpec(mu_lo.shape, const),
        pl.BlockSpec(par.shape, const),
        pl.BlockSpec(wup.shape, const),
        pl.BlockSpec(aup.shape, const),
        pl.BlockSpec(vup.shape, const),
        pl.BlockSpec(bd.shape, const),
        pl.BlockSpec(tri.shape, const),
        pl.BlockSpec(code.shape, const),
    ]
    args = [pb3, pb3, pb3, pb3, pf3, mu4, mu_lo, par, wup, aup, vup, bd, tri, code]
    out_sds = jax.ShapeDtypeStruct((batch, seq, w), BF16)
    if first_layer:
        out_specs = [row_spec, row_spec]
        out_shape = [out_sds, jax.ShapeDtypeStruct((batch, seq, w), F32)]
    else:
        in_specs.append(row_spec)
        args.append(v_first)
        out_specs = row_spec
        out_shape = out_sds
    return pl.pallas_call(
        functools.partial(_rwkv_kernel, first_layer=first_layer),
        grid=(seq // (STEP_CHUNKS * CHUNK),),
        in_specs=in_specs,
        out_specs=out_specs,
        out_shape=out_shape,
        scratch_shapes=[
            pltpu.VMEM((batch, SUBLANES, w), F32),
            pltpu.VMEM((batch, SUBLANES, wl), F32),
            pltpu.VMEM((batch, w // LANES, LANES, LANES), F32),
        ],
        compiler_params=pltpu.CompilerParams(
            dimension_semantics=("arbitrary",), vmem_limit_bytes=VMEM_LIMIT),
        name="rwkv7_branch_first" if first_layer else "rwkv7_branch",
    )(*args)


def _out_kernel(x_ref, ohg_ref, orw_ref, ghg_ref, grw_ref, gate_ref, wa_ref, wb_ref, wo_ref,
                lng_ref, lnb_ref, o_ref, *, alpha):
    d = functools.partial(lax.dot_general, dimension_numbers=NN, preferred_element_type=F32)
    y_hg = d(ohg_ref[...].astype(BF16), wa_ref[...])
    y_rw = d(orw_ref[...].astype(BF16), wb_ref[...])
    merged = _sigmoid(ghg_ref[...].astype(F32)) * y_hg + _sigmoid(grw_ref[...].astype(F32)) * y_rw
    out = d(merged.astype(BF16), wo_ref[...])
    xn = alpha * x_ref[...] + (1.0 + gate_ref[0]) * out
    mu = jnp.mean(xn, axis=-1, keepdims=True)
    xc = xn - mu
    var = jnp.mean(xc * xc, axis=-1, keepdims=True)
    o_ref[...] = xc * lax.rsqrt(var + LN_EPS) * lng_ref[...] + lnb_ref[...]


def _out_call(x2, o_hg, o_rw, proj, gate, wa, wb, wo, ln_g, ln_b, seq, tm, gate_col, alpha):
    bt, d = x2.shape
    wh = o_hg.shape[1]
    per_b = seq // tm
    row = lambda i: (i, 0)
    const = lambda i: (0, 0)
    return pl.pallas_call(
        functools.partial(_out_kernel, alpha=alpha),
        grid=(bt // tm,),
        in_specs=[
            pl.BlockSpec((tm, d), row),
            pl.BlockSpec((tm, wh), row),
            pl.BlockSpec((tm, wh), row),
            pl.BlockSpec((tm, d), lambda i: (i, gate_col)),
            pl.BlockSpec((tm, d), lambda i: (i, gate_col + 1)),
            pl.BlockSpec((1, 1, d), lambda i: (i // per_b, 0, 0)),
            pl.BlockSpec(wa.shape, const),
            pl.BlockSpec(wb.shape, const),
            pl.BlockSpec(wo.shape, const),
            pl.BlockSpec((1, d), const),
            pl.BlockSpec((1, d), const),
        ],
        out_specs=pl.BlockSpec((tm, d), row),
        out_shape=jax.ShapeDtypeStruct((bt, d), F32),
        compiler_params=pltpu.CompilerParams(
            dimension_semantics=("parallel",), vmem_limit_bytes=VMEM_LIMIT),
        name="merge_out_proj_norm",
    )(x2, o_hg, o_rw, proj, proj, gate, wa, wb, wo, ln_g, ln_b)


def kernel(x, c, w_ada, b_ada, w_in, hg_lower_bounds, hg_norm_g, rw_mu, rw_w0, rw_w_up, rw_a0, rw_a_up,
           rw_k_k, rw_k_a, rw_r_k, rw_v0, rw_v_down, rw_v_up, rw_gn_g, rw_gn_b, w_branch_hg, w_branch_rw,
           w_out, ln_g, ln_b):
    batch, seq, d = x.shape
    depth = w_in.shape[0]
    hgw = hg_norm_g.shape[1]
    rww = rw_w0.shape[1]
    r_decay = rw_w_up.shape[1]
    r_icl = rw_a_up.shape[1]
    r_vres = rw_v_up.shape[1]
    hg_cols = 4 * hgw
    rw_main = 4 * rww
    lo_w = 2 * LANES
    assert r_decay + r_icl + r_vres <= lo_w and hgw == rww and d == 2 * hgw
    assert seq % (STEP_CHUNKS * CHUNK) == 0
    alpha = float((2 * depth) ** 0.25)

    lb_soft = jax.nn.softmax(hg_lower_bounds.astype(F32), axis=0)
    lower_bounds = jnp.cumsum(lb_soft, axis=0) - lb_soft[0]

    seg = jnp.asarray(np.tile(_hgrn_segment_matrix(CHUNK), (1, 2)), BF16)
    code = jnp.asarray(_split_code(CHUNK), jnp.int32)
    tri = jnp.asarray(np.tile(np.tril(np.ones((CHUNK, CHUNK), np.float32)), (1, 2)), BF16)
    bd = jnp.asarray(_block_ones(LANES, RW_HEAD), BF16)

    c_pad = jnp.zeros((8, d), F32).at[:batch].set(c)
    cond = _ada_call(c_pad, w_ada, b_ada)

    x2 = x.reshape(batch * seq, d)
    tm = min(1024, seq)
    v_first = None
    for l in range(depth):
        shift = cond[l, :batch, 0:d].reshape(batch, 1, d)
        scale = cond[l, :batch, d:2 * d].reshape(batch, 1, d)
        gate = cond[l, :batch, 2 * d:3 * d].reshape(batch, 1, d)

        wl = w_in[l]
        w_q, w_f, w_i, w_z = (wl[:, j * hgw:(j + 1) * hgw] for j in range(4))
        w_rw = wl[:, hg_cols:hg_cols + rw_main]
        w_lo = wl[:, hg_cols + rw_main:hg_cols + rw_main + r_decay + r_icl]
        w_gate = wl[:, hg_cols + rw_main + r_decay + r_icl:]
        w_vd = rw_v_down[l - 1] if l > 0 else jnp.zeros((d, r_vres), F32)
        pad = jnp.zeros((d, lo_w - (r_decay + r_icl + r_vres)), F32)
        w_f32 = jnp.concatenate([w_q, w_f, w_lo, w_vd, pad], axis=1).astype(BF16)
        w_b16 = jnp.concatenate([w_i, w_z, w_gate, w_rw], axis=1).astype(BF16)
        assert w_b16.shape[1] % (2 * LANES) == 0
        proj_f, proj_b = _proj_call(x2, shift, scale, w_f32, w_b16, seq, tm, w_b16.shape[1] // 2)

        pf3 = proj_f.reshape(batch, seq, w_f32.shape[1])
        pb3 = proj_b.reshape(batch, seq, w_b16.shape[1])
        o_hg = _hgrn_call(pf3, pb3, lower_bounds[l].reshape(1, hgw), hg_norm_g[l].reshape(1, hgw), seg, code)

        mu_l = rw_mu[l]
        mu4 = jnp.zeros((8, rww), F32).at[0:4].set(mu_l[:rw_main].reshape(4, rww))
        mu_lo = jnp.zeros((1, lo_w), F32).at[0, :r_decay + r_icl].set(mu_l[rw_main:])
        v0 = rw_v0[l - 1] if l > 0 else jnp.zeros((rww,), F32)
        par = jnp.stack([rw_w0[l], rw_a0[l], rw_k_k[l], rw_k_a[l], rw_r_k[l].reshape(rww),
                         rw_gn_g[l], rw_gn_b[l], v0], axis=0)
        wup = jnp.zeros((lo_w, rww), F32).at[0:r_decay].set(rw_w_up[l]).astype(BF16)
        aup = jnp.zeros((lo_w, rww), F32).at[r_decay:r_decay + r_icl].set(rw_a_up[l]).astype(BF16)
        vup_l = rw_v_up[l - 1] if l > 0 else jnp.zeros((r_vres, rww), F32)
        vup = jnp.zeros((lo_w, rww), F32).at[r_decay + r_icl:r_decay + r_icl + r_vres].set(vup_l).astype(BF16)
        col0 = (2 * hgw + w_gate.shape[1]) // rww
        res = _rwkv_call(pf3, pb3, mu4, mu_lo, par, wup, aup, vup, bd, tri, code, v_first, col0, 2 * hgw // lo_w)
        if l == 0:
            o_rw, v_first = res
        else:
            o_rw = res

        x2 = _out_call(x2, o_hg.reshape(batch * seq, hgw), o_rw.reshape(batch * seq, rww), proj_b, gate,
                       w_branch_hg[l].astype(BF16), w_branch_rw[l].astype(BF16), w_out[l].astype(BF16),
                       ln_g[l].reshape(1, d), ln_b[l].reshape(1, d), seq, tm, 2 * hgw // d, alpha)
    return x2.reshape(batch, seq, d)
```

```python
import functools
import math

import numpy as np
import jax
import jax.numpy as jnp
from jax import lax
from jax.experimental import pallas as pl
from jax.experimental.pallas import tpu as pltpu

F32 = jnp.float32
BF16 = jnp.bfloat16

HG_DK = 128
RW_HEAD = 64
LN_EPS = 1e-5
RMS_EPS = 1e-6
GN_EPS = 64e-5
L2_EPS = 1e-12
LB_FLOOR = 1e-30

LANES = 128
SUBLANES = 8
CHUNK = 128
LEVELS = int(math.log2(CHUNK))
OUT_SUBTILES = 4
STEP_CHUNKS = 2
SMALL_LEVELS = int(math.log2(SUBLANES))
VMEM_LIMIT = 56 * 1024 * 1024

NT = (((1,), (1,)), ((), ()))
TN = (((0,), (0,)), ((), ()))
NN = (((1,), (0,)), ((), ()))


def _dot(a, b, dims=NN):
    return lax.dot_general(a.astype(BF16), b.astype(BF16), dims, preferred_element_type=F32)


def _split2(x):
    hi = x.astype(BF16)
    lo = (x - hi.astype(F32)).astype(BF16)
    return hi, lo


def _dot3(a, b):
    ah, al = _split2(a)
    bh, bl = _split2(b)
    d = functools.partial(lax.dot_general, dimension_numbers=NN, preferred_element_type=F32)
    return d(ah, bh) + (d(ah, bl) + d(al, bh))


def _sel_dot(m01_twice, x):
    return lax.dot_general(m01_twice, jnp.concatenate(_split2(x), axis=0), NN, preferred_element_type=F32)


def _sigmoid(x):
    return 1.0 / (1.0 + jnp.exp(-x))


def _silu(x):
    return x * _sigmoid(x)


def _sigmoid_pair(x):
    t = jnp.exp(-jnp.abs(x))
    big = 1.0 / (1.0 + t)
    small = t * big
    pos = x >= 0.0
    return jnp.where(pos, big, small), jnp.where(pos, small, big)


def _split_code(c):
    t = np.arange(c)[:, None]
    s = np.arange(c)[None, :]
    x = t ^ s
    code = np.where(x > 0, np.floor(np.log2(np.maximum(x, 1))).astype(np.int32), -1)
    code = np.where(s > t, -2, code)
    return code.astype(np.int32)


def _hgrn_segment_matrix(c):
    blocks = []
    j = np.arange(c)[None, :]
    t = np.arange(c)[:, None]
    for l in range(SMALL_LEVELS):
        m = 1 << l
        start = (t // (2 * m)) * (2 * m)
        mid = start + m - 1
        second = ((t // m) % 2) == 1
        blk = np.where(second, (j > mid) & (j <= t), (j > t) & (j <= mid))
        blocks.append(blk)
    blocks.append(j <= t)
    return np.concatenate(blocks, axis=0).astype(np.float32)


def _block_ones(n, blk):
    i = np.arange(n)
    return (i[:, None] // blk == i[None, :] // blk).astype(np.float32)


def _ada_kernel(c_ref, w_ref, b_ref, o_ref):
    sc = _silu(c_ref[...])
    o_ref[0] = _dot3(sc, w_ref[0]) + b_ref[0]


def _ada_call(c_pad, w_ada, b_ada):
    depth, d, d3 = w_ada.shape
    rows = c_pad.shape[0]
    tn = d
    return pl.pallas_call(
        _ada_kernel,
        grid=(depth, d3 // tn),
        in_specs=[
            pl.BlockSpec((rows, d), lambda l, j: (0, 0)),
            pl.BlockSpec((1, d, tn), lambda l, j: (l, 0, j)),
            pl.BlockSpec((1, 1, tn), lambda l, j: (l, 0, j)),
        ],
        out_specs=pl.BlockSpec((1, rows, tn), lambda l, j: (l, 0, j)),
        out_shape=jax.ShapeDtypeStruct((depth, rows, d3), F32),
        compiler_params=pltpu.CompilerParams(
            dimension_semantics=("parallel", "parallel"), vmem_limit_bytes=VMEM_LIMIT),
        name="adaln_cond",
    )(c_pad, w_ada, b_ada.reshape(depth, 1, d3))


def _proj_kernel(x_ref, shift_ref, scale_ref, wf_ref, wb_ref, of_ref, ob_ref, h_ref):
    j = pl.program_id(1)

    @pl.when(j == 0)
    def _():
        h = x_ref[...] * (1.0 + scale_ref[0]) + shift_ref[0]
        h_ref[...] = h.astype(BF16)
        of_ref[...] = lax.dot_general(h_ref[...], wf_ref[...], NN, preferred_element_type=F32)

    @pl.when(j > 0)
    def _():
        ob_ref[...] = lax.dot_general(h_ref[...], wb_ref[...], NN, preferred_element_type=F32).astype(BF16)


def _proj_call(x2, shift, scale, w_f, w_b, seq, tm, tn_b):
    bt, d = x2.shape
    n_f, n_b = w_f.shape[1], w_b.shape[1]
    per_b = seq // tm
    b_tile = lambda i, j: (i, jnp.maximum(j - 1, 0))
    return pl.pallas_call(
        _proj_kernel,
        grid=(bt // tm, 1 + n_b // tn_b),
        in_specs=[
            pl.BlockSpec((tm, d), lambda i, j: (i, 0)),
            pl.BlockSpec((1, 1, d), lambda i, j: (i // per_b, 0, 0)),
            pl.BlockSpec((1, 1, d), lambda i, j: (i // per_b, 0, 0)),
            pl.BlockSpec((d, n_f), lambda i, j: (0, 0)),
            pl.BlockSpec((d, tn_b), lambda i, j: (0, jnp.maximum(j - 1, 0))),
        ],
        out_specs=[pl.BlockSpec((tm, n_f), lambda i, j: (i, 0)), pl.BlockSpec((tm, tn_b), b_tile)],
        out_shape=[jax.ShapeDtypeStruct((bt, n_f), F32), jax.ShapeDtypeStruct((bt, n_b), BF16)],
        scratch_shapes=[pltpu.VMEM((tm, d), BF16)],
        compiler_params=pltpu.CompilerParams(
            dimension_semantics=("parallel", "arbitrary"), vmem_limit_bytes=VMEM_LIMIT),
        name="modulate_in_proj",
    )(x2, shift, scale, w_f, w_b)


def _hgrn_kernel(q_ref, f_ref, i_ref, z_ref, lb_ref, g_ref, seg_ref, code_ref, o_ref, st_ref, *, n_heads):
    c = CHUNK
    n_batch, rows, _ = q_ref.shape
    items = [(ci, b) for ci in range(rows // c) for b in range(n_batch)]
    rsl = lambda ci: slice(ci * c, (ci + 1) * c)

    @pl.when(pl.program_id(0) == 0)
    def _():
        st_ref[...] = jnp.zeros_like(st_ref)

    lb = lb_ref[...]
    lb_floor = jnp.maximum(lb, LB_FLOOR)
    code = code_ref[...]
    seg = seg_ref[...]
    g = g_ref[...]

    ks, d_alls = {}, {}
    for ci, b in items:
        sig, sig_neg = _sigmoid_pair(f_ref[b, rsl(ci), :])
        log2_f = jnp.log(lb_floor + (1.0 - lb) * sig) * math.log2(math.e)
        ks[ci, b] = (1.0 - lb) * sig_neg
        d_alls[ci, b] = _sel_dot(seg, log2_f)

    sls = [slice(h * HG_DK, (h + 1) * HG_DK) for h in range(n_heads)]

    def scores(ci, b):
        d_all = d_alls[ci, b]
        nb = d_all[SMALL_LEVELS * c:(SMALL_LEVELS + 1) * c, :]
        e_last = jnp.exp2(nb[c - 1:c, :])
        e_b = jnp.exp2(nb).astype(BF16)
        e_suf = jnp.exp2(nb[c - 1:c, :] - nb).astype(BF16)
        e_lvl = [jnp.exp2(d_all[l * c:(l + 1) * c, :]).astype(BF16) for l in range(SMALL_LEVELS)]
        for l in range(SMALL_LEVELS, LEVELS):
            m = 1 << l
            ref = jnp.concatenate(
                [jnp.broadcast_to(nb[s + m - 1:s + m, :], (2 * m, nb.shape[1])) for s in range(0, c, 2 * m)], axis=0)
            e_lvl.append(jnp.exp2(-jnp.abs(nb - ref)).astype(BF16))
        q, k, v = q_ref[b, rsl(ci), :].astype(BF16), ks[ci, b].astype(BF16), i_ref[b, rsl(ci), :].astype(BF16)
        out = []
        for sl in sls:
            qh, kh = q[:, sl], k[:, sl]
            sc = jnp.where(code == -1, _dot(qh, kh, NT), 0.0)
            for l in range(LEVELS):
                e = e_lvl[l][:, sl]
                sc = jnp.where(code == l, _dot(qh * e, kh * e, NT), sc)
            out.append(dict(sc=sc, v=v[:, sl], q_in=qh * e_b[:, sl], k_out=kh * e_suf[:, sl], e_last=e_last[:, sl]))
        return out

    def finish(ci, b, heads):
        for h, (sl, d) in enumerate(zip(sls, heads)):
            st = st_ref[b, h]
            o = _dot(d["sc"], d["v"]) + _dot(d["q_in"], st, NT)
            st_ref[b, h] = st * d["e_last"] + _dot(d["v"], d["k_out"], TN)
            o = o * lax.rsqrt(jnp.mean(o * o, axis=-1, keepdims=True) + RMS_EPS)
            o_ref[b, rsl(ci), sl] = (o * g[:, sl] * _silu(z_ref[b, rsl(ci), sl].astype(F32))).astype(o_ref.dtype)

    pending = scores(*items[0])
    for prev, cur in zip(items[:-1], items[1:]):
        nxt = scores(*cur)
        finish(*prev, pending)
        pending = nxt
    finish(*items[-1], pending)


def _hgrn_call(pf3, pb3, lb, norm_g, seg, code):
    batch, seq, _ = pf3.shape
    w = lb.shape[-1]
    n_heads = w // HG_DK
    col = lambda j: (lambda t: (0, t, j))
    const = lambda t: (0, 0)
    blk = (batch, STEP_CHUNKS * CHUNK, w)
    return pl.pallas_call(
        functools.partial(_hgrn_kernel, n_heads=n_heads),
        grid=(seq // (STEP_CHUNKS * CHUNK),),
        in_specs=[
            pl.BlockSpec(blk, col(0)),
            pl.BlockSpec(blk, col(1)),
            pl.BlockSpec(blk, col(0)),
            pl.BlockSpec(blk, col(1)),
            pl.BlockSpec((1, w), const),
            pl.BlockSpec((1, w), const),
            pl.BlockSpec(seg.shape, const),
            pl.BlockSpec(code.shape, const),
        ],
        out_specs=pl.BlockSpec(blk, col(0)),
        out_shape=jax.ShapeDtypeStruct((batch, seq, w), BF16),
        scratch_shapes=[pltpu.VMEM((batch, n_heads, HG_DK, HG_DK), F32)],
        compiler_params=pltpu.CompilerParams(
            dimension_semantics=("arbitrary",), vmem_limit_bytes=VMEM_LIMIT),
        name="hgrn2_branch",
    )(pf3, pf3, pb3, pb3, lb, norm_g, seg, code)


def _rwkv_kernel(*refs, first_layer):
    (r_ref, k_ref, v_ref, z_ref, lo_ref, mu_ref, mulo_ref, par_ref, wup_ref, aup_ref, vup_ref,
     bd_ref, tri_ref, code_ref) = refs[:14]
    if first_layer:
        vfirst_ref, (o_ref, vout_ref) = None, refs[14:16]
    else:
        vfirst_ref, o_ref, vout_ref = refs[14], refs[15], None
    prev_ref, prevlo_ref, st_ref = refs[16:]
    c = CHUNK
    n_batch, rows, w = r_ref.shape
    n_chunks = rows // c
    n_pairs = w // LANES

    @pl.when(pl.program_id(0) == 0)
    def _():
        prev_ref[...] = jnp.zeros_like(prev_ref)
        prevlo_ref[...] = jnp.zeros_like(prevlo_ref)
        st_ref[...] = jnp.zeros_like(st_ref)

    row0 = lax.broadcasted_iota(jnp.int32, (c, 1), 0) == 0
    bd = bd_ref[...]
    code = code_ref[...]
    tri = tri_ref[...]
    lane = lax.broadcasted_iota(jnp.int32, (1, LANES), 1)
    m_a = lane < RW_HEAD
    row_i = lax.broadcasted_iota(jnp.int32, (LANES, LANES), 0)
    col_i = lax.broadcasted_iota(jnp.int32, (LANES, LANES), 1)
    same_head = (row_i < RW_HEAD) == (col_i < RW_HEAD)
    strict = code >= 0
    incl = code >= -1
    eye = jnp.where(code == -1, 1.0, 0.0)
    sls = [slice(p * LANES, (p + 1) * LANES) for p in range(n_pairs)]

    def head_sum(x):
        return _dot(x, bd)

    def shifted(p, prev_row, m):
        prev = jnp.where(row0, prev_row, pltpu.roll(p, 1, 0))
        return p + (prev - p) * m

    def load_shifted(ref, ci, b, sl, carry_ref, carry_row, m):
        rs = slice(ci * c, (ci + 1) * c)
        cur = ref[b, rs, sl].astype(F32)
        if ci == 0:
            prev_row = carry_ref[b, carry_row:carry_row + 1, sl]
        else:
            prev_row = ref[b, ci * c - 1:ci * c, sl].astype(F32)
        if ci == n_chunks - 1:
            carry_ref[b, carry_row:carry_row + 1, sl] = cur[c - 1:c, :]
        return shifted(cur, prev_row, m)

    lo_cache = {}

    def low_rank_inputs(ci, b):
        if (ci, b) not in lo_cache:
            lo = load_shifted(lo_ref, ci, b, slice(None), prevlo_ref, 0, mulo_ref[...])
            lo_cache[ci, b] = (jnp.tanh(lo).astype(BF16), lo.astype(BF16))
        return lo_cache[ci, b]

    def prep(ci, b, p, pre):
        sl = sls[p]
        w0, a0, k_k, k_a, r_k, _, _, v0 = (par_ref[i:i + 1, sl] for i in range(8))
        rs = slice(ci * c, (ci + 1) * c)
        tanh_lo, lo = low_rank_inputs(ci, b)
        r, k, v, z = (load_shifted(ref, ci, b, sl, prev_ref, i, mu_ref[i:i + 1, sl])
                      for i, ref in enumerate((r_ref, k_ref, v_ref, z_ref)))
        gate = _silu(z)
        kk = k * k_k
        kk_sq = (kk * kk).astype(BF16)
        yield
        w_lin = lax.dot_general(tanh_lo, wup_ref[:, sl], NN, preferred_element_type=F32)
        a_lin = lax.dot_general(lo, aup_ref[:, sl], NN, preferred_element_type=F32)
        if not first_layer:
            v_lin = lax.dot_general(lo, vup_ref[:, sl], NN, preferred_element_type=F32)
        kk_ss = lax.dot_general(kk_sq, bd, NN, preferred_element_type=F32)
        yield
        lw = -math.exp(-0.5) * _sigmoid(w0 + w_lin)
        a = _sigmoid(a0 + a_lin)
        if first_layer:
            vout_ref[b, rs, sl] = v
        else:
            v = v + (vfirst_ref[b, rs, sl] - v) * _sigmoid(v0 + v_lin)
        kk = kk / jnp.maximum(jnp.sqrt(kk_ss), L2_EPS)
        k = k * (1.0 + (a - 1.0) * k_a)
        be = kk * a
        rkr = (r * k * r_k).astype(BF16)
        yield
        bonus_sum = lax.dot_general(rkr, bd, NN, preferred_element_type=F32)
        cum = _sel_dot(tri, lw)
        yield
        bonus = bonus_sum * v
        cum_prev = cum - lw
        c_mid = cum[c // 2 - 1:c // 2, :]
        c_end = cum[c - 1:c, :]
        e_mid_bwd = jnp.exp(c_mid - cum)
        e_end = jnp.exp(c_end - cum)
        pre[ci, b, p] = dict(
            r_abs=(r * jnp.exp(cum)).astype(BF16), kn_abs=(kk * jnp.exp(cum_prev)).astype(BF16),
            r_mid=(r * jnp.exp(cum - c_mid)).astype(BF16), kn_mid=(kk * jnp.exp(cum_prev - c_mid)).astype(BF16),
            k_mid=(k * e_mid_bwd).astype(BF16), b_mid=(be * e_mid_bwd).astype(BF16),
            k_end=(k * e_end).astype(BF16), b_end=(be * e_end).astype(BF16),
            w_end=jnp.exp(c_end), v=v.astype(BF16), bonus=bonus, gate=gate)

    def recur(pairs, pre, ys):
        n = len(pairs)
        heads = [(i, hh) for i in range(n) for hh in range(2)]
        ch = [pre[bp] for bp in pairs]
        gs = []
        for d in ch:
            zero = jnp.zeros_like(d["kn_mid"])
            lhs = jnp.concatenate([jnp.where(m_a, d["kn_mid"], zero), jnp.where(m_a, zero, d["kn_mid"]),
                                   jnp.where(m_a, d["r_mid"], zero), jnp.where(m_a, zero, d["r_mid"])], axis=0)
            rhs = jnp.concatenate([d["k_mid"], d["b_mid"]], axis=0)
            gs.append(_dot(lhs, rhs, NT))
        yield
        sts = [st_ref[b, p] for _, b, p in pairs]
        lows = [jnp.where(strict, gs[i][hh * c:(hh + 1) * c, c:2 * c], 0.0) for i, hh in heads]
        strict2 = jnp.concatenate([strict, strict], axis=0)
        a1v = [_dot(jnp.where(strict2, gs[i][0:2 * c, 0:c], 0.0), ch[i]["v"]) for i in range(n)]
        sk = [_dot(jnp.concatenate([ch[i]["kn_abs"], ch[i]["r_abs"]], axis=0), sts[i], NT) for i in range(n)]
        x1 = [sk[i][0:c, :] + jnp.where(m_a, a1v[i][0:c, :], a1v[i][c:2 * c, :]) for i in range(n)]
        yield
        ts = [eye - jnp.where(code == 0, lo_h, 0.0) for lo_h in lows]
        for l in range(1, LEVELS):
            m = 1 << l
            offs = [jnp.where(code == l, lo_h, 0.0) for lo_h in lows]
            if m < SUBLANES:
                us = [_dot(t, off) for t, off in zip(ts, offs)]
                yield
                ts = [t - _dot(u, t) for t, u in zip(ts, us)]
                yield
            else:
                lower = [jnp.concatenate([t[s + m:s + 2 * m, :] for s in range(0, c, 2 * m)], axis=0) for t in ts]
                us = [_dot(lo_rows, off) for lo_rows, off in zip(lower, offs)]
                yield
                new = [lo_rows - _dot(u, t) for lo_rows, u, t in zip(lower, us, ts)]
                ts = [jnp.concatenate(
                    [piece for j, s in enumerate(range(0, c, 2 * m))
                     for piece in (t[s:s + m, :], nw[j * m:(j + 1) * m, :])], axis=0) for t, nw in zip(ts, new)]
                yield
        tx = [_dot(jnp.concatenate([ts[2 * i], ts[2 * i + 1]], axis=0), x1[i]) for i in range(n)]
        sa = [(-jnp.where(m_a, tx[i][0:c, :], tx[i][c:2 * c, :])).astype(BF16) for i in range(n)]
        vs = [jnp.concatenate([ch[i]["v"], sa[i]], axis=0) for i in range(n)]
        yield
        incl2 = jnp.concatenate([incl, incl], axis=1)
        yh = [_dot(jnp.where(jnp.concatenate([incl2, incl2], axis=0), gs[i][2 * c:4 * c, :], 0.0), vs[i])
              for i in range(n)]
        yield
        for i, (ci, b, p) in enumerate(pairs):
            zed = jnp.concatenate([ch[i]["k_end"], ch[i]["b_end"]], axis=0)
            st_ref[b, p] = sts[i] * ch[i]["w_end"] + jnp.where(same_head, _dot(vs[i], zed, TN), 0.0)
            ys[ci, b, p] = sk[i][c:2 * c, :] + jnp.where(m_a, yh[i][0:c, :], yh[i][c:2 * c, :])
        yield

    def finish(ci, b, p, y, d):
        sl = sls[p]
        inv_n = 1.0 / RW_HEAD
        y_sum = head_sum(y)
        yield
        yc = y - y_sum * inv_n
        yc_sq = (yc * yc).astype(BF16)
        yield
        var = lax.dot_general(yc_sq, bd, NN, preferred_element_type=F32) * inv_n
        yield
        yn = yc * lax.rsqrt(var + GN_EPS) * par_ref[5:6, sl] + par_ref[6:7, sl]
        o_ref[b, ci * c:(ci + 1) * c, sl] = ((yn + d["bonus"]) * d["gate"]).astype(o_ref.dtype)

    def run_all(gens):
        gens = list(gens)
        while gens:
            gens = [g for g in gens if next(g, StopIteration) is not StopIteration]

    group = 2 if n_batch % 2 == 0 else 1
    groups = [[(ci, b, p) for b in range(g0, g0 + group) for p in range(n_pairs)]
              for ci in range(n_chunks) for g0 in range(0, n_batch, group)]
    pre, ys = {}, {}
    run_all(prep(*bp, pre) for bp in groups[0])
    for gi, pairs in enumerate(groups):
        fill = [prep(*bp, pre) for bp in (groups[gi + 1] if gi + 1 < len(groups) else [])]
        fill += [finish(*bp, ys[bp], pre[bp]) for bp in (groups[gi - 1] if gi > 0 else [])]
        n_stages = 2 * LEVELS + 2
        start = [(j * (n_stages - 4)) // max(len(fill), 1) for j in range(len(fill))]
        active = []
        for s, _ in enumerate(recur(pairs, pre, ys)):
            active += [g for j, g in enumerate(fill) if start[j] == s]
            active = [g for g in active if next(g, StopIteration) is not StopIteration]
        run_all(active)
    run_all(finish(*bp, ys[bp], pre[bp]) for bp in groups[-1])


def _rwkv_call(pf3, pb3, mu4, mu_lo, par, wup, aup, vup, bd, tri, code, v_first, col0, lo_col):
    batch, seq, _ = pf3.shape
    w = par.shape[-1]
    wl = mu_lo.shape[-1]
    first_layer = v_first is None
    col = lambda j: (lambda t: (0, t, j))
    const = lambda t: (0, 0)
    blk = (batch, STEP_CHUNKS * CHUNK, w)
    row_spec = pl.BlockSpec(blk, col(0))
    in_specs = [
        pl.BlockSpec(blk, col(col0)),
        pl.BlockSpec(blk, col(col0 + 1)),
        pl.BlockSpec(blk, col(col0 + 2)),
        pl.BlockSpec(blk, col(col0 + 3)),
        pl.BlockSpec((batch, STEP_CHUNKS * CHUNK, wl), col(lo_col)),
        pl.BlockSpec(mu4.shape, const),
        pl.BlockSpec(mu_lo.shape, const),
        pl.BlockSpec(par.shape, const),
        pl.BlockSpec(wup.shape, const),
        pl.BlockSpec(aup.shape, const),
        pl.BlockSpec(vup.shape, const),
        pl.BlockSpec(bd.shape, const),
        pl.BlockSpec(tri.shape, const),
        pl.BlockSpec(code.shape, const),
    ]
    args = [pb3, pb3, pb3, pb3, pf3, mu4, mu_lo, par, wup, aup, vup, bd, tri, code]
    out_sds = jax.ShapeDtypeStruct((batch, seq, w), BF16)
    if first_layer:
        out_specs = [row_spec, row_spec]
        out_shape = [out_sds, jax.ShapeDtypeStruct((batch, seq, w), F32)]
    else:
        in_specs.append(row_spec)
        args.append(v_first)
        out_specs = row_spec
        out_shape = out_sds
    return pl.pallas_call(
        functools.partial(_rwkv_kernel, first_layer=first_layer),
        grid=(seq // (STEP_CHUNKS * CHUNK),),
        in_specs=in_specs,
        out_specs=out_specs,
        out_shape=out_shape,
        scratch_shapes=[
            pltpu.VMEM((batch, SUBLANES, w), F32),
            pltpu.VMEM((batch, SUBLANES, wl), F32),
            pltpu.VMEM((batch, w // LANES, LANES, LANES), F32),
        ],
        compiler_params=pltpu.CompilerParams(
            dimension_semantics=("arbitrary",), vmem_limit_bytes=VMEM_LIMIT),
        name="rwkv7_branch_first" if first_layer else "rwkv7_branch",
    )(*args)


def _out_kernel(x_ref, ohg_ref, orw_ref, ghg_ref, grw_ref, gate_ref, wa_ref, wb_ref, wo_ref,
                lng_ref, lnb_ref, o_ref, *, alpha):
    d = functools.partial(lax.dot_general, dimension_numbers=NN, preferred_element_type=F32)
    rows = x_ref.shape[0] // OUT_SUBTILES

    def branches(k):
        rs = slice(k * rows, (k + 1) * rows)
        return d(ohg_ref[rs, :], wa_ref[...]), d(orw_ref[rs, :], wb_ref[...])

    def merge(k, y_hg, y_rw):
        rs = slice(k * rows, (k + 1) * rows)
        merged = _sigmoid(ghg_ref[rs, :].astype(F32)) * y_hg + _sigmoid(grw_ref[rs, :].astype(F32)) * y_rw
        return merged.astype(BF16)

    def tail(k, merged):
        rs = slice(k * rows, (k + 1) * rows)
        xn = alpha * x_ref[rs, :] + (1.0 + gate_ref[0]) * d(merged, wo_ref[...])
        mu = jnp.mean(xn, axis=-1, keepdims=True)
        xc = xn - mu
        var = jnp.mean(xc * xc, axis=-1, keepdims=True)
        o_ref[rs, :] = xc * lax.rsqrt(var + LN_EPS) * lng_ref[...] + lnb_ref[...]

    ys = branches(0)
    for k in range(OUT_SUBTILES):
        nxt = branches(k + 1) if k + 1 < OUT_SUBTILES else None
        tail(k, merge(k, *ys))
        ys = nxt


def _out_call(x2, o_hg, o_rw, proj, gate, wa, wb, wo, ln_g, ln_b, seq, tm, gate_col, alpha):
    bt, d = x2.shape
    wh = o_hg.shape[1]
    per_b = seq // tm
    row = lambda i: (i, 0)
    const = lambda i: (0, 0)
    return pl.pallas_call(
        functools.partial(_out_kernel, alpha=alpha),
        grid=(bt // tm,),
        in_specs=[
            pl.BlockSpec((tm, d), row),
            pl.BlockSpec((tm, wh), row),
            pl.BlockSpec((tm, wh), row),
            pl.BlockSpec((tm, d), lambda i: (i, gate_col)),
            pl.BlockSpec((tm, d), lambda i: (i, gate_col + 1)),
            pl.BlockSpec((1, 1, d), lambda i: (i // per_b, 0, 0)),
            pl.BlockSpec(wa.shape, const),
            pl.BlockSpec(wb.shape, const),
            pl.BlockSpec(wo.shape, const),
            pl.BlockSpec((1, d), const),
            pl.BlockSpec((1, d), const),
        ],
        out_specs=pl.BlockSpec((tm, d), row),
        out_shape=jax.ShapeDtypeStruct((bt, d), F32),
        compiler_params=pltpu.CompilerParams(
            dimension_semantics=("parallel",), vmem_limit_bytes=VMEM_LIMIT),
        name="merge_out_proj_norm",
    )(x2, o_hg, o_rw, proj, proj, gate, wa, wb, wo, ln_g, ln_b)


def kernel(x, c, w_ada, b_ada, w_in, hg_lower_bounds, hg_norm_g, rw_mu, rw_w0, rw_w_up, rw_a0, rw_a_up,
           rw_k_k, rw_k_a, rw_r_k, rw_v0, rw_v_down, rw_v_up, rw_gn_g, rw_gn_b, w_branch_hg, w_branch_rw,
           w_out, ln_g, ln_b):
    batch, seq, d = x.shape
    depth = w_in.shape[0]
    hgw = hg_norm_g.shape[1]
    rww = rw_w0.shape[1]
    r_decay = rw_w_up.shape[1]
    r_icl = rw_a_up.shape[1]
    r_vres = rw_v_up.shape[1]
    hg_cols = 4 * hgw
    rw_main = 4 * rww
    lo_w = 2 * LANES
    assert r_decay + r_icl + r_vres <= lo_w and hgw == rww and d == 2 * hgw
    assert seq % (STEP_CHUNKS * CHUNK) == 0
    alpha = float((2 * depth) ** 0.25)

    lb_soft = jax.nn.softmax(hg_lower_bounds.astype(F32), axis=0)
    lower_bounds = jnp.cumsum(lb_soft, axis=0) - lb_soft[0]

    seg = jnp.asarray(np.tile(_hgrn_segment_matrix(CHUNK), (1, 2)), BF16)
    code = jnp.asarray(_split_code(CHUNK), jnp.int32)
    tri = jnp.asarray(np.tile(np.tril(np.ones((CHUNK, CHUNK), np.float32)), (1, 2)), BF16)
    bd = jnp.asarray(_block_ones(LANES, RW_HEAD), BF16)

    c_pad = jnp.zeros((8, d), F32).at[:batch].set(c)
    cond = _ada_call(c_pad, w_ada, b_ada)

    x2 = x.reshape(batch * seq, d)
    tm = min(1024, seq)
    v_first = None
    for l in range(depth):
        shift = cond[l, :batch, 0:d].reshape(batch, 1, d)
        scale = cond[l, :batch, d:2 * d].reshape(batch, 1, d)
        gate = cond[l, :batch, 2 * d:3 * d].reshape(batch, 1, d)

        wl = w_in[l]
        w_q, w_f, w_i, w_z = (wl[:, j * hgw:(j + 1) * hgw] for j in range(4))
        w_rw = wl[:, hg_cols:hg_cols + rw_main]
        w_lo = wl[:, hg_cols + rw_main:hg_cols + rw_main + r_decay + r_icl]
        w_gate = wl[:, hg_cols + rw_main + r_decay + r_icl:]
        w_vd = rw_v_down[l - 1] if l > 0 else jnp.zeros((d, r_vres), F32)
        pad = jnp.zeros((d, lo_w - (r_decay + r_icl + r_vres)), F32)
        w_f32 = jnp.concatenate([w_q, w_f, w_lo, w_vd, pad], axis=1).astype(BF16)
        w_b16 = jnp.concatenate([w_i, w_z, w_gate, w_rw], axis=1).astype(BF16)
        assert w_b16.shape[1] % (2 * LANES) == 0
        proj_f, proj_b = _proj_call(x2, shift, scale, w_f32, w_b16, seq, tm, w_b16.shape[1] // 2)

        pf3 = proj_f.reshape(batch, seq, w_f32.shape[1])
        pb3 = proj_b.reshape(batch, seq, w_b16.shape[1])
        o_hg = _hgrn_call(pf3, pb3, lower_bounds[l].reshape(1, hgw), hg_norm_g[l].reshape(1, hgw), seg, code)

        mu_l = rw_mu[l]
        mu4 = jnp.zeros((8, rww), F32).at[0:4].set(mu_l[:rw_main].reshape(4, rww))
        mu_lo = jnp.zeros((1, lo_w), F32).at[0, :r_decay + r_icl].set(mu_l[rw_main:])
        v0 = rw_v0[l - 1] if l > 0 else jnp.zeros((rww,), F32)
        par = jnp.stack([rw_w0[l], rw_a0[l], rw_k_k[l], rw_k_a[l], rw_r_k[l].reshape(rww),
                         rw_gn_g[l], rw_gn_b[l], v0], axis=0)
        wup = jnp.zeros((lo_w, rww), F32).at[0:r_decay].set(rw_w_up[l]).astype(BF16)
        aup = jnp.zeros((lo_w, rww), F32).at[r_decay:r_decay + r_icl].set(rw_a_up[l]).astype(BF16)
        vup_l = rw_v_up[l - 1] if l > 0 else jnp.zeros((r_vres, rww), F32)
        vup = jnp.zeros((lo_w, rww), F32).at[r_decay + r_icl:r_decay + r_icl + r_vres].set(vup_l).astype(BF16)
        col0 = (2 * hgw + w_gate.shape[1]) // rww
        res = _rwkv_call(pf3, pb3, mu4, mu_lo, par, wup, aup, vup, bd, tri, code, v_first, col0, 2 * hgw // lo_w)
        if l == 0:
            o_rw, v_first = res
        else:
            o_rw = res

        x2 = _out_call(x2, o_hg.reshape(batch * seq, hgw), o_rw.reshape(batch * seq, rww), proj_b, gate,
                       w_branch_hg[l].astype(BF16), w_branch_rw[l].astype(BF16), w_out[l].astype(BF16),
                       ln_g[l].reshape(1, d), ln_b[l].reshape(1, d), seq, tm, 2 * hgw // d, alpha)
    return x2.reshape(batch, seq, d)
```

```python
import functools
import math

import numpy as np
import jax
import jax.numpy as jnp
from jax import lax
from jax.experimental import pallas as pl
from jax.experimental.pallas import tpu as pltpu

F32 = jnp.float32
BF16 = jnp.bfloat16

HG_DK = 128
RW_HEAD = 64
LN_EPS = 1e-5
RMS_EPS = 1e-6
GN_EPS = 64e-5
L2_EPS = 1e-12
LB_FLOOR = 1e-30

LANES = 128
SUBLANES = 8
CHUNK = 128
LEVELS = int(math.log2(CHUNK))
ROW_TILE = 1024
STEP_CHUNKS = 2
SMALL_LEVELS = int(math.log2(SUBLANES))
VMEM_LIMIT = 56 * 1024 * 1024

NT = (((1,), (1,)), ((), ()))
TN = (((0,), (0,)), ((), ()))
NN = (((1,), (0,)), ((), ()))


def _dot(a, b, dims=NN):
    return lax.dot_general(a.astype(BF16), b.astype(BF16), dims, preferred_element_type=F32)


def _split2(x):
    hi = x.astype(BF16)
    lo = (x - hi.astype(F32)).astype(BF16)
    return hi, lo


def _dot3(a, b):
    ah, al = _split2(a)
    bh, bl = _split2(b)
    d = functools.partial(lax.dot_general, dimension_numbers=NN, preferred_element_type=F32)
    return d(ah, bh) + (d(ah, bl) + d(al, bh))


def _sel_dot(m01_twice, x):
    return lax.dot_general(m01_twice, jnp.concatenate(_split2(x), axis=0), NN, preferred_element_type=F32)


def _sigmoid(x):
    return 1.0 / (1.0 + jnp.exp(-x))


def _silu(x):
    return x * _sigmoid(x)


def _sigmoid_pair(x):
    t = jnp.exp(-jnp.abs(x))
    big = 1.0 / (1.0 + t)
    small = t * big
    pos = x >= 0.0
    return jnp.where(pos, big, small), jnp.where(pos, small, big)


def _split_code(c):
    t = np.arange(c)[:, None]
    s = np.arange(c)[None, :]
    x = t ^ s
    code = np.where(x > 0, np.floor(np.log2(np.maximum(x, 1))).astype(np.int32), -1)
    code = np.where(s > t, -2, code)
    return code.astype(np.int32)


def _hgrn_segment_matrix(c):
    blocks = []
    j = np.arange(c)[None, :]
    t = np.arange(c)[:, None]
    for l in range(SMALL_LEVELS):
        m = 1 << l
        start = (t // (2 * m)) * (2 * m)
        mid = start + m - 1
        second = ((t // m) % 2) == 1
        blk = np.where(second, (j > mid) & (j <= t), (j > t) & (j <= mid))
        blocks.append(blk)
    blocks.append(j <= t)
    return np.concatenate(blocks, axis=0).astype(np.float32)


def _block_ones(n, blk):
    i = np.arange(n)
    return (i[:, None] // blk == i[None, :] // blk).astype(np.float32)


def _ada_kernel(c_ref, w_ref, b_ref, o_ref):
    sc = _silu(c_ref[...])
    o_ref[0] = _dot3(sc, w_ref[0]) + b_ref[0]


def _ada_call(c_pad, w_ada, b_ada):
    depth, d, d3 = w_ada.shape
    rows = c_pad.shape[0]
    tn = d
    return pl.pallas_call(
        _ada_kernel,
        grid=(depth, d3 // tn),
        in_specs=[
            pl.BlockSpec((rows, d), lambda l, j: (0, 0)),
            pl.BlockSpec((1, d, tn), lambda l, j: (l, 0, j)),
            pl.BlockSpec((1, 1, tn), lambda l, j: (l, 0, j)),
        ],
        out_specs=pl.BlockSpec((1, rows, tn), lambda l, j: (l, 0, j)),
        out_shape=jax.ShapeDtypeStruct((depth, rows, d3), F32),
        compiler_params=pltpu.CompilerParams(
            dimension_semantics=("parallel", "parallel"), vmem_limit_bytes=VMEM_LIMIT),
        name="adaln_cond",
    )(c_pad, w_ada, b_ada.reshape(depth, 1, d3))


def _proj_kernel(x_ref, shift_ref, scale_ref, wf_ref, wb_ref, of_ref, ob_ref, h_ref):
    j = pl.program_id(1)

    @pl.when(j == 0)
    def _():
        h = x_ref[...] * (1.0 + scale_ref[0]) + shift_ref[0]
        h_ref[...] = h.astype(BF16)
        of_ref[...] = lax.dot_general(h_ref[...], wf_ref[...], NN, preferred_element_type=F32)

    @pl.when(j > 0)
    def _():
        ob_ref[...] = lax.dot_general(h_ref[...], wb_ref[...], NN, preferred_element_type=F32).astype(BF16)


def _proj_call(x2, shift, scale, w_f, w_b, seq, tm, tn_b):
    bt, d = x2.shape
    n_f, n_b = w_f.shape[1], w_b.shape[1]
    per_b = seq // tm
    b_tile = lambda i, j: (i, jnp.maximum(j - 1, 0))
    return pl.pallas_call(
        _proj_kernel,
        grid=(bt // tm, 1 + n_b // tn_b),
        in_specs=[
            pl.BlockSpec((tm, d), lambda i, j: (i, 0)),
            pl.BlockSpec((1, 1, d), lambda i, j: (i // per_b, 0, 0)),
            pl.BlockSpec((1, 1, d), lambda i, j: (i // per_b, 0, 0)),
            pl.BlockSpec((d, n_f), lambda i, j: (0, 0)),
            pl.BlockSpec((d, tn_b), lambda i, j: (0, jnp.maximum(j - 1, 0))),
        ],
        out_specs=[pl.BlockSpec((tm, n_f), lambda i, j: (i, 0)), pl.BlockSpec((tm, tn_b), b_tile)],
        out_shape=[jax.ShapeDtypeStruct((bt, n_f), F32), jax.ShapeDtypeStruct((bt, n_b), BF16)],
        scratch_shapes=[pltpu.VMEM((tm, d), BF16)],
        compiler_params=pltpu.CompilerParams(
            dimension_semantics=("parallel", "arbitrary"), vmem_limit_bytes=VMEM_LIMIT),
        name="modulate_in_proj",
    )(x2, shift, scale, w_f, w_b)


def _hgrn_kernel(q_ref, f_ref, i_ref, z_ref, lb_ref, g_ref, seg_ref, code_ref, o_ref, st_ref, *, n_heads):
    c = CHUNK
    n_batch, rows, _ = q_ref.shape
    items = [(ci, b) for ci in range(rows // c) for b in range(n_batch)]
    rsl = lambda ci: slice(ci * c, (ci + 1) * c)

    @pl.when(pl.program_id(0) == 0)
    def _():
        st_ref[...] = jnp.zeros_like(st_ref)

    lb = lb_ref[...]
    lb_floor = jnp.maximum(lb, LB_FLOOR)
    code = code_ref[...]
    seg = seg_ref[...]
    g = g_ref[...]

    ks, d_alls = {}, {}
    for ci, b in items:
        sig, sig_neg = _sigmoid_pair(f_ref[b, rsl(ci), :])
        log2_f = jnp.log(lb_floor + (1.0 - lb) * sig) * math.log2(math.e)
        ks[ci, b] = (1.0 - lb) * sig_neg
        d_alls[ci, b] = _sel_dot(seg, log2_f)

    sls = [slice(h * HG_DK, (h + 1) * HG_DK) for h in range(n_heads)]

    def scores(ci, b):
        d_all = d_alls[ci, b]
        nb = d_all[SMALL_LEVELS * c:(SMALL_LEVELS + 1) * c, :]
        e_last = jnp.exp2(nb[c - 1:c, :])
        e_b = jnp.exp2(nb).astype(BF16)
        e_suf = jnp.exp2(nb[c - 1:c, :] - nb).astype(BF16)
        e_lvl = [jnp.exp2(d_all[l * c:(l + 1) * c, :]).astype(BF16) for l in range(SMALL_LEVELS)]
        for l in range(SMALL_LEVELS, LEVELS):
            m = 1 << l
            ref = jnp.concatenate(
                [jnp.broadcast_to(nb[s + m - 1:s + m, :], (2 * m, nb.shape[1])) for s in range(0, c, 2 * m)], axis=0)
            e_lvl.append(jnp.exp2(-jnp.abs(nb - ref)).astype(BF16))
        q, k, v = q_ref[b, rsl(ci), :].astype(BF16), ks[ci, b].astype(BF16), i_ref[b, rsl(ci), :].astype(BF16)
        out = []
        for sl in sls:
            qh, kh = q[:, sl], k[:, sl]
            sc = jnp.where(code == -1, _dot(qh, kh, NT), 0.0)
            for l in range(LEVELS):
                e = e_lvl[l][:, sl]
                sc = jnp.where(code == l, _dot(qh * e, kh * e, NT), sc)
            out.append(dict(sc=sc, v=v[:, sl], q_in=qh * e_b[:, sl], k_out=kh * e_suf[:, sl], e_last=e_last[:, sl]))
        return out

    def finish(ci, b, heads):
        for h, (sl, d) in enumerate(zip(sls, heads)):
            st = st_ref[b, h]
            o = _dot(d["sc"], d["v"]) + _dot(d["q_in"], st, NT)
            st_ref[b, h] = st * d["e_last"] + _dot(d["v"], d["k_out"], TN)
            o = o * lax.rsqrt(jnp.mean(o * o, axis=-1, keepdims=True) + RMS_EPS)
            o_ref[b, rsl(ci), sl] = (o * g[:, sl] * _silu(z_ref[b, rsl(ci), sl].astype(F32))).astype(o_ref.dtype)

    pending = scores(*items[0])
    for prev, cur in zip(items[:-1], items[1:]):
        nxt = scores(*cur)
        finish(*prev, pending)
        pending = nxt
    finish(*items[-1], pending)


def _hgrn_call(pf3, pb3, lb, norm_g, seg, code):
    batch, seq, _ = pf3.shape
    w = lb.shape[-1]
    n_heads = w // HG_DK
    col = lambda j: (lambda t: (0, t, j))
    const = lambda t: (0, 0)
    blk = (batch, STEP_CHUNKS * CHUNK, w)
    return pl.pallas_call(
        functools.partial(_hgrn_kernel, n_heads=n_heads),
        grid=(seq // (STEP_CHUNKS * CHUNK),),
        in_specs=[
            pl.BlockSpec(blk, col(0)),
            pl.BlockSpec(blk, col(1)),
            pl.BlockSpec(blk, col(0)),
            pl.BlockSpec(blk, col(1)),
            pl.BlockSpec((1, w), const),
            pl.BlockSpec((1, w), const),
            pl.BlockSpec(seg.shape, const),
            pl.BlockSpec(code.shape, const),
        ],
        out_specs=pl.BlockSpec(blk, col(0)),
        out_shape=jax.ShapeDtypeStruct((batch, seq, w), BF16),
        scratch_shapes=[pltpu.VMEM((batch, n_heads, HG_DK, HG_DK), F32)],
        compiler_params=pltpu.CompilerParams(
            dimension_semantics=("arbitrary",), vmem_limit_bytes=VMEM_LIMIT),
        name="hgrn2_branch",
    )(pf3, pf3, pb3, pb3, lb, norm_g, seg, code)


def _rwkv_kernel(*refs, first_layer):
    (r_ref, k_ref, v_ref, z_ref, lo_ref, mu_ref, mulo_ref, par_ref, wup_ref, aup_ref, vup_ref,
     bd_ref, tri_ref, code_ref) = refs[:14]
    if first_layer:
        vfirst_ref, (o_ref, vout_ref) = None, refs[14:16]
    else:
        vfirst_ref, o_ref, vout_ref = refs[14], refs[15], None
    prev_ref, prevlo_ref, st_ref = refs[16:]
    c = CHUNK
    n_batch, rows, w = r_ref.shape
    n_chunks = rows // c
    n_pairs = w // LANES

    @pl.when(pl.program_id(0) == 0)
    def _():
        prev_ref[...] = jnp.zeros_like(prev_ref)
        prevlo_ref[...] = jnp.zeros_like(prevlo_ref)
        st_ref[...] = jnp.zeros_like(st_ref)

    row0 = lax.broadcasted_iota(jnp.int32, (c, 1), 0) == 0
    bd = bd_ref[...]
    code = code_ref[...]
    tri = tri_ref[...]
    lane = lax.broadcasted_iota(jnp.int32, (1, LANES), 1)
    m_a = lane < RW_HEAD
    row_i = lax.broadcasted_iota(jnp.int32, (LANES, LANES), 0)
    col_i = lax.broadcasted_iota(jnp.int32, (LANES, LANES), 1)
    same_head = (row_i < RW_HEAD) == (col_i < RW_HEAD)
    strict = code >= 0
    incl = code >= -1
    eye = jnp.where(code == -1, 1.0, 0.0)
    sls = [slice(p * LANES, (p + 1) * LANES) for p in range(n_pairs)]

    def head_sum(x):
        return _dot(x, bd)

    def shifted(p, prev_row, m):
        prev = jnp.where(row0, prev_row, pltpu.roll(p, 1, 0))
        return p + (prev - p) * m

    def load_shifted(ref, ci, b, sl, carry_ref, carry_row, m):
        rs = slice(ci * c, (ci + 1) * c)
        cur = ref[b, rs, sl].astype(F32)
        if ci == 0:
            prev_row = carry_ref[b, carry_row:carry_row + 1, sl]
        else:
            prev_row = ref[b, ci * c - 1:ci * c, sl].astype(F32)
        if ci == n_chunks - 1:
            carry_ref[b, carry_row:carry_row + 1, sl] = cur[c - 1:c, :]
        return shifted(cur, prev_row, m)

    lo_cache = {}

    def low_rank_inputs(ci, b):
        if (ci, b) not in lo_cache:
            lo = load_shifted(lo_ref, ci, b, slice(None), prevlo_ref, 0, mulo_ref[...])
            lo_cache[ci, b] = (jnp.tanh(lo).astype(BF16), lo.astype(BF16))
        return lo_cache[ci, b]

    def prep(ci, b, p, pre):
        sl = sls[p]
        w0, a0, k_k, k_a, r_k, _, _, v0 = (par_ref[i:i + 1, sl] for i in range(8))
        rs = slice(ci * c, (ci + 1) * c)
        tanh_lo, lo = low_rank_inputs(ci, b)
        r, k, v, z = (load_shifted(ref, ci, b, sl, prev_ref, i, mu_ref[i:i + 1, sl])
                      for i, ref in enumerate((r_ref, k_ref, v_ref, z_ref)))
        gate = _silu(z)
        kk = k * k_k
        kk_sq = (kk * kk).astype(BF16)
        yield
        w_lin = lax.dot_general(tanh_lo, wup_ref[:, sl], NN, preferred_element_type=F32)
        a_lin = lax.dot_general(lo, aup_ref[:, sl], NN, preferred_element_type=F32)
        if not first_layer:
            v_lin = lax.dot_general(lo, vup_ref[:, sl], NN, preferred_element_type=F32)
        yield
        lw = -math.exp(-0.5) * _sigmoid(w0 + w_lin)
        a = _sigmoid(a0 + a_lin)
        if first_layer:
            vout_ref[b, rs, sl] = v
        else:
            v = v + (vfirst_ref[b, rs, sl] - v) * _sigmoid(v0 + v_lin)
        k = k * (1.0 + (a - 1.0) * k_a)
        rkr = (r * k * r_k).astype(BF16)
        yield
        sums = lax.dot_general(jnp.concatenate([kk_sq, rkr], axis=0), bd, NN, preferred_element_type=F32)
        cum = _sel_dot(tri, lw)
        yield
        kk = kk / jnp.maximum(jnp.sqrt(sums[0:c, :]), L2_EPS)
        be = kk * a
        bonus = sums[c:2 * c, :] * v
        cum_prev = cum - lw
        c_mid = cum[c // 2 - 1:c // 2, :]
        c_end = cum[c - 1:c, :]
        e_mid_bwd = jnp.exp(c_mid - cum)
        e_end = jnp.exp(c_end - cum)
        pre[ci, b, p] = dict(
            r_abs=(r * jnp.exp(cum)).astype(BF16), kn_abs=(kk * jnp.exp(cum_prev)).astype(BF16),
            r_mid=(r * jnp.exp(cum - c_mid)).astype(BF16), kn_mid=(kk * jnp.exp(cum_prev - c_mid)).astype(BF16),
            k_mid=(k * e_mid_bwd).astype(BF16), b_mid=(be * e_mid_bwd).astype(BF16),
            k_end=(k * e_end).astype(BF16), b_end=(be * e_end).astype(BF16),
            w_end=jnp.exp(c_end), v=v.astype(BF16), bonus=bonus, gate=gate)

    def recur(pairs, pre, ys):
        n = len(pairs)
        heads = [(i, hh) for i in range(n) for hh in range(2)]
        ch = [pre[bp] for bp in pairs]
        gs = []
        for d in ch:
            zero = jnp.zeros_like(d["kn_mid"])
            lhs = jnp.concatenate([jnp.where(m_a, d["kn_mid"], zero), jnp.where(m_a, zero, d["kn_mid"]),
                                   jnp.where(m_a, d["r_mid"], zero), jnp.where(m_a, zero, d["r_mid"])], axis=0)
            rhs = jnp.concatenate([d["k_mid"], d["b_mid"]], axis=0)
            gs.append(_dot(lhs, rhs, NT))
        yield
        sts = [st_ref[b, p] for _, b, p in pairs]
        lows = [jnp.where(strict, gs[i][hh * c:(hh + 1) * c, c:2 * c], 0.0) for i, hh in heads]
        strict2 = jnp.concatenate([strict, strict], axis=0)
        a1v = [_dot(jnp.where(strict2, gs[i][0:2 * c, 0:c], 0.0), ch[i]["v"]) for i in range(n)]
        sk = [_dot(jnp.concatenate([ch[i]["kn_abs"], ch[i]["r_abs"]], axis=0), sts[i], NT) for i in range(n)]
        x1 = [sk[i][0:c, :] + jnp.where(m_a, a1v[i][0:c, :], a1v[i][c:2 * c, :]) for i in range(n)]
        yield
        ts = [eye - jnp.where(code == 0, lo_h, 0.0) for lo_h in lows]
        for l in range(1, LEVELS):
            m = 1 << l
            offs = [jnp.where(code == l, lo_h, 0.0) for lo_h in lows]
            if m < SUBLANES:
                us = [_dot(t, off) for t, off in zip(ts, offs)]
                yield
                ts = [t - _dot(u, t) for t, u in zip(ts, us)]
                yield
            else:
                lower = [jnp.concatenate([t[s + m:s + 2 * m, :] for s in range(0, c, 2 * m)], axis=0) for t in ts]
                us = [_dot(lo_rows, off) for lo_rows, off in zip(lower, offs)]
                yield
                new = [lo_rows - _dot(u, t) for lo_rows, u, t in zip(lower, us, ts)]
                ts = [jnp.concatenate(
                    [piece for j, s in enumerate(range(0, c, 2 * m))
                     for piece in (t[s:s + m, :], nw[j * m:(j + 1) * m, :])], axis=0) for t, nw in zip(ts, new)]
                yield
        tx = [_dot(jnp.concatenate([ts[2 * i], ts[2 * i + 1]], axis=0), x1[i]) for i in range(n)]
        sa = [(-jnp.where(m_a, tx[i][0:c, :], tx[i][c:2 * c, :])).astype(BF16) for i in range(n)]
        vs = [jnp.concatenate([ch[i]["v"], sa[i]], axis=0) for i in range(n)]
        yield
        incl2 = jnp.concatenate([incl, incl], axis=1)
        yh = [_dot(jnp.where(jnp.concatenate([incl2, incl2], axis=0), gs[i][2 * c:4 * c, :], 0.0), vs[i])
              for i in range(n)]
        yield
        for i, (ci, b, p) in enumerate(pairs):
            zed = jnp.concatenate([ch[i]["k_end"], ch[i]["b_end"]], axis=0)
            st_ref[b, p] = sts[i] * ch[i]["w_end"] + jnp.where(same_head, _dot(vs[i], zed, TN), 0.0)
            ys[ci, b, p] = sk[i][c:2 * c, :] + jnp.where(m_a, yh[i][0:c, :], yh[i][c:2 * c, :])
        yield

    def finish(ci, b, p, y, d):
        sl = sls[p]
        inv_n = 1.0 / RW_HEAD
        y_sum = head_sum(y)
        yield
        yc = y - y_sum * inv_n
        yc_sq = (yc * yc).astype(BF16)
        yield
        var = lax.dot_general(yc_sq, bd, NN, preferred_element_type=F32) * inv_n
        yield
        yn = yc * lax.rsqrt(var + GN_EPS) * par_ref[5:6, sl] + par_ref[6:7, sl]
        o_ref[b, ci * c:(ci + 1) * c, sl] = ((yn + d["bonus"]) * d["gate"]).astype(o_ref.dtype)

    def run_all(gens):
        gens = list(gens)
        while gens:
            gens = [g for g in gens if next(g, StopIteration) is not StopIteration]

    group = 2 if n_batch % 2 == 0 else 1
    groups = [[(ci, b, p) for b in range(g0, g0 + group) for p in range(n_pairs)]
              for ci in range(n_chunks) for g0 in range(0, n_batch, group)]
    pre, ys = {}, {}
    run_all(prep(*bp, pre) for bp in groups[0])
    for gi, pairs in enumerate(groups):
        fill = [prep(*bp, pre) for bp in (groups[gi + 1] if gi + 1 < len(groups) else [])]
        fill += [finish(*bp, ys[bp], pre[bp]) for bp in (groups[gi - 1] if gi > 0 else [])]
        n_stages = 2 * LEVELS + 2
        start = [(j * (n_stages - 4)) // max(len(fill), 1) for j in range(len(fill))]
        active = []
        for s, _ in enumerate(recur(pairs, pre, ys)):
            active += [g for j, g in enumerate(fill) if start[j] == s]
            active = [g for g in active if next(g, StopIteration) is not StopIteration]
        run_all(active)
    run_all(finish(*bp, ys[bp], pre[bp]) for bp in groups[-1])


def _rwkv_call(pf3, pb3, mu4, mu_lo, par, wup, aup, vup, bd, tri, code, v_first, col0, lo_col):
    batch, seq, _ = pf3.shape
    w = par.shape[-1]
    wl = mu_lo.shape[-1]
    first_layer = v_first is None
    col = lambda j: (lambda t: (0, t, j))
    const = lambda t: (0, 0)
    blk = (batch, STEP_CHUNKS * CHUNK, w)
    row_spec = pl.BlockSpec(blk, col(0))
    in_specs = [
        pl.BlockSpec(blk, col(col0)),
        pl.BlockSpec(blk, col(col0 + 1)),
        pl.BlockSpec(blk, col(col0 + 2)),
        pl.BlockSpec(blk, col(col0 + 3)),
        pl.BlockSpec((batch, STEP_CHUNKS * CHUNK, wl), col(lo_col)),
        pl.BlockSpec(mu4.shape, const),
        pl.BlockSpec(mu_lo.shape, const),
        pl.BlockSpec(par.shape, const),
        pl.BlockSpec(wup.shape, const),
        pl.BlockSpec(aup.shape, const),
        pl.BlockSpec(vup.shape, const),
        pl.BlockSpec(bd.shape, const),
        pl.BlockSpec(tri.shape, const),
        pl.BlockSpec(code.shape, const),
    ]
    args = [pb3, pb3, pb3, pb3, pf3, mu4, mu_lo, par, wup, aup, vup, bd, tri, code]
    out_sds = jax.ShapeDtypeStruct((batch, seq, w), BF16)
    if first_layer:
        out_specs = [row_spec, row_spec]
        out_shape = [out_sds, jax.ShapeDtypeStruct((batch, seq, w), F32)]
    else:
        in_specs.append(row_spec)
        args.append(v_first)
        out_specs = row_spec
        out_shape = out_sds
    return pl.pallas_call(
        functools.partial(_rwkv_kernel, first_layer=first_layer),
        grid=(seq // (STEP_CHUNKS * CHUNK),),
        in_specs=in_specs,
        out_specs=out_specs,
        out_shape=out_shape,
        scratch_shapes=[
            pltpu.VMEM((batch, SUBLANES, w), F32),
            pltpu.VMEM((batch, SUBLANES, wl), F32),
            pltpu.VMEM((batch, w // LANES, LANES, LANES), F32),
        ],
        compiler_params=pltpu.CompilerParams(
            dimension_semantics=("arbitrary",), vmem_limit_bytes=VMEM_LIMIT),
        name="rwkv7_branch_first" if first_layer else "rwkv7_branch",
    )(*args)


def _out_kernel(x_ref, ohg_ref, orw_ref, ghg_ref, grw_ref, gate_ref, wa_ref, wb_ref, wo_ref,
                lng_ref, lnb_ref, o_ref, *, alpha):
    d = functools.partial(lax.dot_general, dimension_numbers=NN, preferred_element_type=F32)
    y_hg = d(ohg_ref[...], wa_ref[...])
    y_rw = d(orw_ref[...], wb_ref[...])
    merged = _sigmoid(ghg_ref[...].astype(F32)) * y_hg + _sigmoid(grw_ref[...].astype(F32)) * y_rw
    out = d(merged.astype(BF16), wo_ref[...])
    xn = alpha * x_ref[...] + (1.0 + gate_ref[0]) * out
    mu = jnp.mean(xn, axis=-1, keepdims=True)
    xc = xn - mu
    var = jnp.mean(xc * xc, axis=-1, keepdims=True)
    o_ref[...] = xc * lax.rsqrt(var + LN_EPS) * lng_ref[...] + lnb_ref[...]


def _out_call(x2, o_hg, o_rw, proj, gate, wa, wb, wo, ln_g, ln_b, seq, tm, gate_col, alpha):
    bt, d = x2.shape
    wh = o_hg.shape[1]
    per_b = seq // tm
    row = lambda i: (i, 0)
    const = lambda i: (0, 0)
    return pl.pallas_call(
        functools.partial(_out_kernel, alpha=alpha),
        grid=(bt // tm,),
        in_specs=[
            pl.BlockSpec((tm, d), row),
            pl.BlockSpec((tm, wh), row),
            pl.BlockSpec((tm, wh), row),
            pl.BlockSpec((tm, d), lambda i: (i, gate_col)),
            pl.BlockSpec((tm, d), lambda i: (i, gate_col + 1)),
            pl.BlockSpec((1, 1, d), lambda i: (i // per_b, 0, 0)),
            pl.BlockSpec(wa.shape, const),
            pl.BlockSpec(wb.shape, const),
            pl.BlockSpec(wo.shape, const),
            pl.BlockSpec((1, d), const),
            pl.BlockSpec((1, d), const),
        ],
        out_specs=pl.BlockSpec((tm, d), row),
        out_shape=jax.ShapeDtypeStruct((bt, d), F32),
        compiler_params=pltpu.CompilerParams(
            dimension_semantics=("parallel",), vmem_limit_bytes=VMEM_LIMIT),
        name="merge_out_proj_norm",
    )(x2, o_hg, o_rw, proj, proj, gate, wa, wb, wo, ln_g, ln_b)


def kernel(x, c, w_ada, b_ada, w_in, hg_lower_bounds, hg_norm_g, rw_mu, rw_w0, rw_w_up, rw_a0, rw_a_up,
           rw_k_k, rw_k_a, rw_r_k, rw_v0, rw_v_down, rw_v_up, rw_gn_g, rw_gn_b, w_branch_hg, w_branch_rw,
           w_out, ln_g, ln_b):
    batch, seq, d = x.shape
    depth = w_in.shape[0]
    hgw = hg_norm_g.shape[1]
    rww = rw_w0.shape[1]
    r_decay = rw_w_up.shape[1]
    r_icl = rw_a_up.shape[1]
    r_vres = rw_v_up.shape[1]
    hg_cols = 4 * hgw
    rw_main = 4 * rww
    lo_w = 2 * LANES
    assert r_decay + r_icl + r_vres <= lo_w and hgw == rww and d == 2 * hgw
    assert seq % (STEP_CHUNKS * CHUNK) == 0
    alpha = float((2 * depth) ** 0.25)

    lb_soft = jax.nn.softmax(hg_lower_bounds.astype(F32), axis=0)
    lower_bounds = jnp.cumsum(lb_soft, axis=0) - lb_soft[0]

    seg = jnp.asarray(np.tile(_hgrn_segment_matrix(CHUNK), (1, 2)), BF16)
    code = jnp.asarray(_split_code(CHUNK), jnp.int32)
    tri = jnp.asarray(np.tile(np.tril(np.ones((CHUNK, CHUNK), np.float32)), (1, 2)), BF16)
    bd = jnp.asarray(_block_ones(LANES, RW_HEAD), BF16)

    c_pad = jnp.zeros((SUBLANES, d), F32).at[:batch].set(c)
    cond = _ada_call(c_pad, w_ada, b_ada)

    x2 = x.reshape(batch * seq, d)
    tm = min(ROW_TILE, seq)
    v_first = None
    for l in range(depth):
        shift = cond[l, :batch, 0:d].reshape(batch, 1, d)
        scale = cond[l, :batch, d:2 * d].reshape(batch, 1, d)
        gate = cond[l, :batch, 2 * d:3 * d].reshape(batch, 1, d)

        wl = w_in[l]
        w_q, w_f, w_i, w_z = (wl[:, j * hgw:(j + 1) * hgw] for j in range(4))
        w_rw = wl[:, hg_cols:hg_cols + rw_main]
        w_lo = wl[:, hg_cols + rw_main:hg_cols + rw_main + r_decay + r_icl]
        w_gate = wl[:, hg_cols + rw_main + r_decay + r_icl:]
        w_vd = rw_v_down[l - 1] if l > 0 else jnp.zeros((d, r_vres), F32)
        pad = jnp.zeros((d, lo_w - (r_decay + r_icl + r_vres)), F32)
        w_f32 = jnp.concatenate([w_q, w_f, w_lo, w_vd, pad], axis=1).astype(BF16)
        w_b16 = jnp.concatenate([w_i, w_z, w_gate, w_rw], axis=1).astype(BF16)
        assert w_b16.shape[1] % (2 * LANES) == 0
        proj_f, proj_b = _proj_call(x2, shift, scale, w_f32, w_b16, seq, tm, w_b16.shape[1] // 2)

        pf3 = proj_f.reshape(batch, seq, w_f32.shape[1])
        pb3 = proj_b.reshape(batch, seq, w_b16.shape[1])
        o_hg = _hgrn_call(pf3, pb3, lower_bounds[l].reshape(1, hgw), hg_norm_g[l].reshape(1, hgw), seg, code)

        mu_l = rw_mu[l]
        mu4 = jnp.zeros((SUBLANES, rww), F32).at[0:4].set(mu_l[:rw_main].reshape(4, rww))
        mu_lo = jnp.zeros((1, lo_w), F32).at[0, :r_decay + r_icl].set(mu_l[rw_main:])
        v0 = rw_v0[l - 1] if l > 0 else jnp.zeros((rww,), F32)
        par = jnp.stack([rw_w0[l], rw_a0[l], rw_k_k[l], rw_k_a[l], rw_r_k[l].reshape(rww),
                         rw_gn_g[l], rw_gn_b[l], v0], axis=0)
        wup = jnp.zeros((lo_w, rww), F32).at[0:r_decay].set(rw_w_up[l]).astype(BF16)
        aup = jnp.zeros((lo_w, rww), F32).at[r_decay:r_decay + r_icl].set(rw_a_up[l]).astype(BF16)
        vup_l = rw_v_up[l - 1] if l > 0 else jnp.zeros((r_vres, rww), F32)
        vup = jnp.zeros((lo_w, rww), F32).at[r_decay + r_icl:r_decay + r_icl + r_vres].set(vup_l).astype(BF16)
        col0 = (2 * hgw + w_gate.shape[1]) // rww
        res = _rwkv_call(pf3, pb3, mu4, mu_lo, par, wup, aup, vup, bd, tri, code, v_first, col0, 2 * hgw // lo_w)
        if l == 0:
            o_rw, v_first = res
        else:
            o_rw = res

        x2 = _out_call(x2, o_hg.reshape(batch * seq, hgw), o_rw.reshape(batch * seq, rww), proj_b, gate,
                       w_branch_hg[l].astype(BF16), w_branch_rw[l].astype(BF16), w_out[l].astype(BF16),
                       ln_g[l].reshape(1, d), ln_b[l].reshape(1, d), seq, tm, 2 * hgw // d, alpha)
    return x2.reshape(batch, seq, d)
```

```python
import functools
import math

import numpy as np
import jax
import jax.numpy as jnp
from jax import lax
from jax.experimental import pallas as pl
from jax.experimental.pallas import tpu as pltpu

F32 = jnp.float32
BF16 = jnp.bfloat16

HG_DK = 128
RW_HEAD = 64
LN_EPS = 1e-5
RMS_EPS = 1e-6
GN_EPS = 64e-5
L2_EPS = 1e-12
LB_FLOOR = 1e-30

LANES = 128
SUBLANES = 8
CHUNK = 128
LEVELS = int(math.log2(CHUNK))
ROW_TILE = 1024
IN_ROW_TILE = 512
STEP_CHUNKS = 2
SMALL_LEVELS = int(math.log2(SUBLANES))
VMEM_LIMIT = 56 * 1024 * 1024

NT = (((1,), (1,)), ((), ()))
TN = (((0,), (0,)), ((), ()))
NN = (((1,), (0,)), ((), ()))


def _dot(a, b, dims=NN):
    return lax.dot_general(a.astype(BF16), b.astype(BF16), dims, preferred_element_type=F32)


def _split2(x):
    hi = x.astype(BF16)
    lo = (x - hi.astype(F32)).astype(BF16)
    return hi, lo


def _dot3(a, b):
    ah, al = _split2(a)
    bh, bl = _split2(b)
    d = functools.partial(lax.dot_general, dimension_numbers=NN, preferred_element_type=F32)
    return d(ah, bh) + (d(ah, bl) + d(al, bh))


def _sel_dot(m01_twice, x):
    return lax.dot_general(m01_twice, jnp.concatenate(_split2(x), axis=0), NN, preferred_element_type=F32)


def _sigmoid(x):
    return 1.0 / (1.0 + jnp.exp(-x))


def _silu(x):
    return x * _sigmoid(x)


def _sigmoid_pair(x):
    t = jnp.exp(-jnp.abs(x))
    big = 1.0 / (1.0 + t)
    small = t * big
    pos = x >= 0.0
    return jnp.where(pos, big, small), jnp.where(pos, small, big)


def _split_code(c):
    t = np.arange(c)[:, None]
    s = np.arange(c)[None, :]
    x = t ^ s
    code = np.where(x > 0, np.floor(np.log2(np.maximum(x, 1))).astype(np.int32), -1)
    code = np.where(s > t, -2, code)
    return code.astype(np.int32)


def _hgrn_segment_matrix(c):
    blocks = []
    j = np.arange(c)[None, :]
    t = np.arange(c)[:, None]
    for l in range(SMALL_LEVELS):
        m = 1 << l
        start = (t // (2 * m)) * (2 * m)
        mid = start + m - 1
        second = ((t // m) % 2) == 1
        blk = np.where(second, (j > mid) & (j <= t), (j > t) & (j <= mid))
        blocks.append(blk)
    blocks.append(j <= t)
    return np.concatenate(blocks, axis=0).astype(np.float32)


def _block_ones(n, blk):
    i = np.arange(n)
    return (i[:, None] // blk == i[None, :] // blk).astype(np.float32)


def _ada_kernel(c_ref, w_ref, b_ref, o_ref):
    sc = _silu(c_ref[...])
    o_ref[0] = _dot3(sc, w_ref[0]) + b_ref[0]


def _ada_call(c_pad, w_ada, b_ada):
    depth, d, d3 = w_ada.shape
    rows = c_pad.shape[0]
    tn = d
    return pl.pallas_call(
        _ada_kernel,
        grid=(depth, d3 // tn),
        in_specs=[
            pl.BlockSpec((rows, d), lambda l, j: (0, 0)),
            pl.BlockSpec((1, d, tn), lambda l, j: (l, 0, j)),
            pl.BlockSpec((1, 1, tn), lambda l, j: (l, 0, j)),
        ],
        out_specs=pl.BlockSpec((1, rows, tn), lambda l, j: (l, 0, j)),
        out_shape=jax.ShapeDtypeStruct((depth, rows, d3), F32),
        compiler_params=pltpu.CompilerParams(
            dimension_semantics=("parallel", "parallel"), vmem_limit_bytes=VMEM_LIMIT),
        name="adaln_cond",
    )(c_pad, w_ada, b_ada.reshape(depth, 1, d3))


def _proj_kernel(x_ref, shift_ref, scale_ref, wf_ref, wb_ref, of_ref, ob_ref, *, tn_b):
    h = (x_ref[...] * (1.0 + scale_ref[0]) + shift_ref[0]).astype(BF16)
    of_ref[...] = lax.dot_general(h, wf_ref[...], NN, preferred_element_type=F32)
    for j in range(wb_ref.shape[1] // tn_b):
        cs = slice(j * tn_b, (j + 1) * tn_b)
        ob_ref[:, cs] = lax.dot_general(h, wb_ref[:, cs], NN, preferred_element_type=F32).astype(BF16)


def _proj_call(x2, shift, scale, w_f, w_b, seq, tm, tn_b):
    bt, d = x2.shape
    n_f, n_b = w_f.shape[1], w_b.shape[1]
    per_b = seq // tm
    return pl.pallas_call(
        functools.partial(_proj_kernel, tn_b=tn_b),
        grid=(bt // tm,),
        in_specs=[
            pl.BlockSpec((tm, d), lambda i: (i, 0)),
            pl.BlockSpec((1, 1, d), lambda i: (i // per_b, 0, 0)),
            pl.BlockSpec((1, 1, d), lambda i: (i // per_b, 0, 0)),
            pl.BlockSpec((d, n_f), lambda i: (0, 0)),
            pl.BlockSpec((d, n_b), lambda i: (0, 0)),
        ],
        out_specs=[pl.BlockSpec((tm, n_f), lambda i: (i, 0)), pl.BlockSpec((tm, n_b), lambda i: (i, 0))],
        out_shape=[jax.ShapeDtypeStruct((bt, n_f), F32), jax.ShapeDtypeStruct((bt, n_b), BF16)],
        compiler_params=pltpu.CompilerParams(
            dimension_semantics=("parallel",), vmem_limit_bytes=VMEM_LIMIT),
        name="modulate_in_proj",
    )(x2, shift, scale, w_f, w_b)


def _hgrn_kernel(q_ref, f_ref, i_ref, z_ref, lb_ref, g_ref, seg_ref, code_ref, o_ref, st_ref, *, n_heads):
    c = CHUNK
    n_batch, rows, _ = q_ref.shape
    items = [(ci, b) for ci in range(rows // c) for b in range(n_batch)]
    rsl = lambda ci: slice(ci * c, (ci + 1) * c)

    @pl.when(pl.program_id(0) == 0)
    def _():
        st_ref[...] = jnp.zeros_like(st_ref)

    lb = lb_ref[...]
    lb_floor = jnp.maximum(lb, LB_FLOOR)
    code = code_ref[...]
    seg = seg_ref[...]
    g = g_ref[...]

    ks, d_alls = {}, {}
    for ci, b in items:
        sig, sig_neg = _sigmoid_pair(f_ref[b, rsl(ci), :])
        log2_f = jnp.log(lb_floor + (1.0 - lb) * sig) * math.log2(math.e)
        ks[ci, b] = (1.0 - lb) * sig_neg
        d_alls[ci, b] = _sel_dot(seg, log2_f)

    sls = [slice(h * HG_DK, (h + 1) * HG_DK) for h in range(n_heads)]

    def scores(ci, b):
        d_all = d_alls[ci, b]
        nb = d_all[SMALL_LEVELS * c:(SMALL_LEVELS + 1) * c, :]
        e_last = jnp.exp2(nb[c - 1:c, :])
        e_b = jnp.exp2(nb).astype(BF16)
        e_suf = jnp.exp2(nb[c - 1:c, :] - nb).astype(BF16)
        e_lvl = [jnp.exp2(d_all[l * c:(l + 1) * c, :]).astype(BF16) for l in range(SMALL_LEVELS)]
        for l in range(SMALL_LEVELS, LEVELS):
            m = 1 << l
            ref = jnp.concatenate(
                [jnp.broadcast_to(nb[s + m - 1:s + m, :], (2 * m, nb.shape[1])) for s in range(0, c, 2 * m)], axis=0)
            e_lvl.append(jnp.exp2(-jnp.abs(nb - ref)).astype(BF16))
        q, k, v = q_ref[b, rsl(ci), :].astype(BF16), ks[ci, b].astype(BF16), i_ref[b, rsl(ci), :].astype(BF16)
        out = []
        for sl in sls:
            qh, kh = q[:, sl], k[:, sl]
            sc = jnp.where(code == -1, _dot(qh, kh, NT), 0.0)
            for l in range(LEVELS):
                e = e_lvl[l][:, sl]
                sc = jnp.where(code == l, _dot(qh * e, kh * e, NT), sc)
            out.append(dict(sc=sc, v=v[:, sl], q_in=qh * e_b[:, sl], k_out=kh * e_suf[:, sl], e_last=e_last[:, sl]))
        return out

    def finish(ci, b, heads):
        for h, (sl, d) in enumerate(zip(sls, heads)):
            st = st_ref[b, h]
            o = _dot(d["sc"], d["v"]) + _dot(d["q_in"], st, NT)
            st_ref[b, h] = st * d["e_last"] + _dot(d["v"], d["k_out"], TN)
            o = o * lax.rsqrt(jnp.mean(o * o, axis=-1, keepdims=True) + RMS_EPS)
            o_ref[b, rsl(ci), sl] = (o * g[:, sl] * _silu(z_ref[b, rsl(ci), sl].astype(F32))).astype(o_ref.dtype)

    pending = scores(*items[0])
    for prev, cur in zip(items[:-1], items[1:]):
        nxt = scores(*cur)
        finish(*prev, pending)
        pending = nxt
    finish(*items[-1], pending)


def _hgrn_call(pf3, pb3, lb, norm_g, seg, code):
    batch, seq, _ = pf3.shape
    w = lb.shape[-1]
    n_heads = w // HG_DK
    col = lambda j: (lambda t: (0, t, j))
    const = lambda t: (0, 0)
    blk = (batch, STEP_CHUNKS * CHUNK, w)
    return pl.pallas_call(
        functools.partial(_hgrn_kernel, n_heads=n_heads),
        grid=(seq // (STEP_CHUNKS * CHUNK),),
        in_specs=[
            pl.BlockSpec(blk, col(0)),
            pl.BlockSpec(blk, col(1)),
            pl.BlockSpec(blk, col(0)),
            pl.BlockSpec(blk, col(1)),
            pl.BlockSpec((1, w), const),
            pl.BlockSpec((1, w), const),
            pl.BlockSpec(seg.shape, const),
            pl.BlockSpec(code.shape, const),
        ],
        out_specs=pl.BlockSpec(blk, col(0)),
        out_shape=jax.ShapeDtypeStruct((batch, seq, w), BF16),
        scratch_shapes=[pltpu.VMEM((batch, n_heads, HG_DK, HG_DK), F32)],
        compiler_params=pltpu.CompilerParams(
            dimension_semantics=("arbitrary",), vmem_limit_bytes=VMEM_LIMIT),
        name="hgrn2_branch",
    )(pf3, pf3, pb3, pb3, lb, norm_g, seg, code)


def _rwkv_kernel(*refs, first_layer):
    (r_ref, k_ref, v_ref, z_ref, lo_ref, mu_ref, mulo_ref, par_ref, wup_ref, aup_ref, vup_ref,
     bd_ref, tri_ref, code_ref) = refs[:14]
    if first_layer:
        vfirst_ref, (o_ref, vout_ref) = None, refs[14:16]
    else:
        vfirst_ref, o_ref, vout_ref = refs[14], refs[15], None
    prev_ref, prevlo_ref, st_ref = refs[16:]
    c = CHUNK
    n_batch, rows, w = r_ref.shape
    n_chunks = rows // c
    n_pairs = w // LANES

    @pl.when(pl.program_id(0) == 0)
    def _():
        prev_ref[...] = jnp.zeros_like(prev_ref)
        prevlo_ref[...] = jnp.zeros_like(prevlo_ref)
        st_ref[...] = jnp.zeros_like(st_ref)

    row0 = lax.broadcasted_iota(jnp.int32, (c, 1), 0) == 0
    bd = bd_ref[...]
    code = code_ref[...]
    tri = tri_ref[...]
    lane = lax.broadcasted_iota(jnp.int32, (1, LANES), 1)
    m_a = lane < RW_HEAD
    row_i = lax.broadcasted_iota(jnp.int32, (LANES, LANES), 0)
    col_i = lax.broadcasted_iota(jnp.int32, (LANES, LANES), 1)
    same_head = (row_i < RW_HEAD) == (col_i < RW_HEAD)
    strict = code >= 0
    incl = code >= -1
    eye = jnp.where(code == -1, 1.0, 0.0)
    sls = [slice(p * LANES, (p + 1) * LANES) for p in range(n_pairs)]

    def head_sum(x):
        return _dot(x, bd)

    def shifted(p, prev_row, m):
        prev = jnp.where(row0, prev_row, pltpu.roll(p, 1, 0))
        return p + (prev - p) * m

    def load_shifted(ref, ci, b, sl, carry_ref, carry_row, m):
        rs = slice(ci * c, (ci + 1) * c)
        cur = ref[b, rs, sl].astype(F32)
        if ci == 0:
            prev_row = carry_ref[b, carry_row:carry_row + 1, sl]
        else:
            prev_row = ref[b, ci * c - 1:ci * c, sl].astype(F32)
        if ci == n_chunks - 1:
            carry_ref[b, carry_row:carry_row + 1, sl] = cur[c - 1:c, :]
        return shifted(cur, prev_row, m)

    lo_cache = {}

    def low_rank_inputs(ci, b):
        if (ci, b) not in lo_cache:
            lo = load_shifted(lo_ref, ci, b, slice(None), prevlo_ref, 0, mulo_ref[...])
            lo_cache[ci, b] = (jnp.tanh(lo).astype(BF16), lo.astype(BF16))
        return lo_cache[ci, b]

    def prep(ci, b, p, pre):
        sl = sls[p]
        w0, a0, k_k, k_a, r_k, _, _, v0 = (par_ref[i:i + 1, sl] for i in range(8))
        rs = slice(ci * c, (ci + 1) * c)
        tanh_lo, lo = low_rank_inputs(ci, b)
        r, k, v, z = (load_shifted(ref, ci, b, sl, prev_ref, i, mu_ref[i:i + 1, sl])
                      for i, ref in enumerate((r_ref, k_ref, v_ref, z_ref)))
        gate = _silu(z)
        kk = k * k_k
        kk_sq = (kk * kk).astype(BF16)
        yield
        w_lin = lax.dot_general(tanh_lo, wup_ref[:, sl], NN, preferred_element_type=F32)
        a_lin = lax.dot_general(lo, aup_ref[:, sl], NN, preferred_element_type=F32)
        if not first_layer:
            v_lin = lax.dot_general(lo, vup_ref[:, sl], NN, preferred_element_type=F32)
        yield
        lw = -math.exp(-0.5) * _sigmoid(w0 + w_lin)
        a = _sigmoid(a0 + a_lin)
        if first_layer:
            vout_ref[b, rs, sl] = v
        else:
            v = v + (vfirst_ref[b, rs, sl] - v) * _sigmoid(v0 + v_lin)
        k = k * (1.0 + (a - 1.0) * k_a)
        rkr = (r * k * r_k).astype(BF16)
        yield
        sums = lax.dot_general(jnp.concatenate([kk_sq, rkr], axis=0), bd, NN, preferred_element_type=F32)
        cum = _sel_dot(tri, lw)
        yield
        kk = kk / jnp.maximum(jnp.sqrt(sums[0:c, :]), L2_EPS)
        be = kk * a
        bonus = sums[c:2 * c, :] * v
        cum_prev = cum - lw
        c_mid = cum[c // 2 - 1:c // 2, :]
        c_end = cum[c - 1:c, :]
        e_mid_bwd = jnp.exp(c_mid - cum)
        e_end = jnp.exp(c_end - cum)
        pre[ci, b, p] = dict(
            r_abs=(r * jnp.exp(cum)).astype(BF16), kn_abs=(kk * jnp.exp(cum_prev)).astype(BF16),
            r_mid=(r * jnp.exp(cum - c_mid)).astype(BF16), kn_mid=(kk * jnp.exp(cum_prev - c_mid)).astype(BF16),
            k_mid=(k * e_mid_bwd).astype(BF16), b_mid=(be * e_mid_bwd).astype(BF16),
            k_end=(k * e_end).astype(BF16), b_end=(be * e_end).astype(BF16),
            w_end=jnp.exp(c_end), v=v.astype(BF16), bonus=bonus, gate=gate)

    def recur(pairs, pre, ys):
        n = len(pairs)
        heads = [(i, hh) for i in range(n) for hh in range(2)]
        ch = [pre[bp] for bp in pairs]
        gs = []
        for d in ch:
            zero = jnp.zeros_like(d["kn_mid"])
            lhs = jnp.concatenate([jnp.where(m_a, d["kn_mid"], zero), jnp.where(m_a, zero, d["kn_mid"]),
                                   jnp.where(m_a, d["r_mid"], zero), jnp.where(m_a, zero, d["r_mid"])], axis=0)
            rhs = jnp.concatenate([d["k_mid"], d["b_mid"]], axis=0)
            gs.append(_dot(lhs, rhs, NT))
        yield
        sts = [st_ref[b, p] for _, b, p in pairs]
        lows = [jnp.where(strict, gs[i][hh * c:(hh + 1) * c, c:2 * c], 0.0) for i, hh in heads]
        strict2 = jnp.concatenate([strict, strict], axis=0)
        a1v = [_dot(jnp.where(strict2, gs[i][0:2 * c, 0:c], 0.0), ch[i]["v"]) for i in range(n)]
        sk = [_dot(jnp.concatenate([ch[i]["kn_abs"], ch[i]["r_abs"]], axis=0), sts[i], NT) for i in range(n)]
        x1 = [sk[i][0:c, :] + jnp.where(m_a, a1v[i][0:c, :], a1v[i][c:2 * c, :]) for i in range(n)]
        yield
        ts = [eye - jnp.where(code == 0, lo_h, 0.0) for lo_h in lows]
        for l in range(1, LEVELS):
            m = 1 << l
            offs = [jnp.where(code == l, lo_h, 0.0) for lo_h in lows]
            if m < SUBLANES:
                us = [_dot(t, off) for t, off in zip(ts, offs)]
                yield
                ts = [t - _dot(u, t) for t, u in zip(ts, us)]
                yield
            else:
                lower = [jnp.concatenate([t[s + m:s + 2 * m, :] for s in range(0, c, 2 * m)], axis=0) for t in ts]
                us = [_dot(lo_rows, off) for lo_rows, off in zip(lower, offs)]
                yield
                new = [lo_rows - _dot(u, t) for lo_rows, u, t in zip(lower, us, ts)]
                ts = [jnp.concatenate(
                    [piece for j, s in enumerate(range(0, c, 2 * m))
                     for piece in (t[s:s + m, :], nw[j * m:(j + 1) * m, :])], axis=0) for t, nw in zip(ts, new)]
                yield
        tx = [_dot(jnp.concatenate([ts[2 * i], ts[2 * i + 1]], axis=0), x1[i]) for i in range(n)]
        sa = [(-jnp.where(m_a, tx[i][0:c, :], tx[i][c:2 * c, :])).astype(BF16) for i in range(n)]
        vs = [jnp.concatenate([ch[i]["v"], sa[i]], axis=0) for i in range(n)]
        yield
        incl2 = jnp.concatenate([incl, incl], axis=1)
        yh = [_dot(jnp.where(jnp.concatenate([incl2, incl2], axis=0), gs[i][2 * c:4 * c, :], 0.0), vs[i])
              for i in range(n)]
        yield
        for i, (ci, b, p) in enumerate(pairs):
            zed = jnp.concatenate([ch[i]["k_end"], ch[i]["b_end"]], axis=0)
            st_ref[b, p] = sts[i] * ch[i]["w_end"] + jnp.where(same_head, _dot(vs[i], zed, TN), 0.0)
            ys[ci, b, p] = sk[i][c:2 * c, :] + jnp.where(m_a, yh[i][0:c, :], yh[i][c:2 * c, :])
        yield

    def finish(ci, b, p, y, d):
        sl = sls[p]
        inv_n = 1.0 / RW_HEAD
        y_sum = head_sum(y)
        yield
        yc = y - y_sum * inv_n
        yc_sq = (yc * yc).astype(BF16)
        yield
        var = lax.dot_general(yc_sq, bd, NN, preferred_element_type=F32) * inv_n
        yield
        yn = yc * lax.rsqrt(var + GN_EPS) * par_ref[5:6, sl] + par_ref[6:7, sl]
        o_ref[b, ci * c:(ci + 1) * c, sl] = ((yn + d["bonus"]) * d["gate"]).astype(o_ref.dtype)

    def run_all(gens):
        gens = list(gens)
        while gens:
            gens = [g for g in gens if next(g, StopIteration) is not StopIteration]

    group = 2 if n_batch % 2 == 0 else 1
    groups = [[(ci, b, p) for b in range(g0, g0 + group) for p in range(n_pairs)]
              for ci in range(n_chunks) for g0 in range(0, n_batch, group)]
    pre, ys = {}, {}
    run_all(prep(*bp, pre) for bp in groups[0])
    for gi, pairs in enumerate(groups):
        fill = [prep(*bp, pre) for bp in (groups[gi + 1] if gi + 1 < len(groups) else [])]
        fill += [finish(*bp, ys[bp], pre[bp]) for bp in (groups[gi - 1] if gi > 0 else [])]
        n_stages = 2 * LEVELS + 2
        start = [(j * (n_stages - 4)) // max(len(fill), 1) for j in range(len(fill))]
        active = []
        for s, _ in enumerate(recur(pairs, pre, ys)):
            active += [g for j, g in enumerate(fill) if start[j] == s]
            active = [g for g in active if next(g, StopIteration) is not StopIteration]
        run_all(active)
    run_all(finish(*bp, ys[bp], pre[bp]) for bp in groups[-1])


def _rwkv_call(pf3, pb3, mu4, mu_lo, par, wup, aup, vup, bd, tri, code, v_first, col0, lo_col):
    batch, seq, _ = pf3.shape
    w = par.shape[-1]
    wl = mu_lo.shape[-1]
    first_layer = v_first is None
    col = lambda j: (lambda t: (0, t, j))
    const = lambda t: (0, 0)
    blk = (batch, STEP_CHUNKS * CHUNK, w)
    row_spec = pl.BlockSpec(blk, col(0))
    in_specs = [
        pl.BlockSpec(blk, col(col0)),
        pl.BlockSpec(blk, col(col0 + 1)),
        pl.BlockSpec(blk, col(col0 + 2)),
        pl.BlockSpec(blk, col(col0 + 3)),
        pl.BlockSpec((batch, STEP_CHUNKS * CHUNK, wl), col(lo_col)),
        pl.BlockSpec(mu4.shape, const),
        pl.BlockSpec(mu_lo.shape, const),
        pl.BlockSpec(par.shape, const),
        pl.BlockSpec(wup.shape, const),
        pl.BlockSpec(aup.shape, const),
        pl.BlockSpec(vup.shape, const),
        pl.BlockSpec(bd.shape, const),
        pl.BlockSpec(tri.shape, const),
        pl.BlockSpec(code.shape, const),
    ]
    args = [pb3, pb3, pb3, pb3, pf3, mu4, mu_lo, par, wup, aup, vup, bd, tri, code]
    out_sds = jax.ShapeDtypeStruct((batch, seq, w), BF16)
    if first_layer:
        out_specs = [row_spec, row_spec]
        out_shape = [out_sds, jax.ShapeDtypeStruct((batch, seq, w), F32)]
    else:
        in_specs.append(row_spec)
        args.append(v_first)
        out_specs = row_spec
        out_shape = out_sds
    return pl.pallas_call(
        functools.partial(_rwkv_kernel, first_layer=first_layer),
        grid=(seq // (STEP_CHUNKS * CHUNK),),
        in_specs=in_specs,
        out_specs=out_specs,
        out_shape=out_shape,
        scratch_shapes=[
            pltpu.VMEM((batch, SUBLANES, w), F32),
            pltpu.VMEM((batch, SUBLANES, wl), F32),
            pltpu.VMEM((batch, w // LANES, LANES, LANES), F32),
        ],
        compiler_params=pltpu.CompilerParams(
            dimension_semantics=("arbitrary",), vmem_limit_bytes=VMEM_LIMIT),
        name="rwkv7_branch_first" if first_layer else "rwkv7_branch",
    )(*args)


def _out_kernel(x_ref, ohg_ref, orw_ref, ghg_ref, grw_ref, gate_ref, wa_ref, wb_ref, wo_ref,
                lng_ref, lnb_ref, o_ref, *, alpha):
    d = functools.partial(lax.dot_general, dimension_numbers=NN, preferred_element_type=F32)
    y_hg = d(ohg_ref[...], wa_ref[...])
    y_rw = d(orw_ref[...], wb_ref[...])
    merged = _sigmoid(ghg_ref[...].astype(F32)) * y_hg + _sigmoid(grw_ref[...].astype(F32)) * y_rw
    out = d(merged.astype(BF16), wo_ref[...])
    xn = alpha * x_ref[...] + (1.0 + gate_ref[0]) * out
    mu = jnp.mean(xn, axis=-1, keepdims=True)
    xc = xn - mu
    var = jnp.mean(xc * xc, axis=-1, keepdims=True)
    o_ref[...] = xc * lax.rsqrt(var + LN_EPS) * lng_ref[...] + lnb_ref[...]


def _out_call(x2, o_hg, o_rw, proj, gate, wa, wb, wo, ln_g, ln_b, seq, tm, gate_col, alpha):
    bt, d = x2.shape
    wh = o_hg.shape[1]
    per_b = seq // tm
    row = lambda i: (i, 0)
    const = lambda i: (0, 0)
    return pl.pallas_call(
        functools.partial(_out_kernel, alpha=alpha),
        grid=(bt // tm,),
        in_specs=[
            pl.BlockSpec((tm, d), row),
            pl.BlockSpec((tm, wh), row),
            pl.BlockSpec((tm, wh), row),
            pl.BlockSpec((tm, d), lambda i: (i, gate_col)),
            pl.BlockSpec((tm, d), lambda i: (i, gate_col + 1)),
            pl.BlockSpec((1, 1, d), lambda i: (i // per_b, 0, 0)),
            pl.BlockSpec(wa.shape, const),
            pl.BlockSpec(wb.shape, const),
            pl.BlockSpec(wo.shape, const),
            pl.BlockSpec((1, d), const),
            pl.BlockSpec((1, d), const),
        ],
        out_specs=pl.BlockSpec((tm, d), row),
        out_shape=jax.ShapeDtypeStruct((bt, d), F32),
        compiler_params=pltpu.CompilerParams(
            dimension_semantics=("parallel",), vmem_limit_bytes=VMEM_LIMIT),
        name="merge_out_proj_norm",
    )(x2, o_hg, o_rw, proj, proj, gate, wa, wb, wo, ln_g, ln_b)


def kernel(x, c, w_ada, b_ada, w_in, hg_lower_bounds, hg_norm_g, rw_mu, rw_w0, rw_w_up, rw_a0, rw_a_up,
           rw_k_k, rw_k_a, rw_r_k, rw_v0, rw_v_down, rw_v_up, rw_gn_g, rw_gn_b, w_branch_hg, w_branch_rw,
           w_out, ln_g, ln_b):
    batch, seq, d = x.shape
    depth = w_in.shape[0]
    hgw = hg_norm_g.shape[1]
    rww = rw_w0.shape[1]
    r_decay = rw_w_up.shape[1]
    r_icl = rw_a_up.shape[1]
    r_vres = rw_v_up.shape[1]
    hg_cols = 4 * hgw
    rw_main = 4 * rww
    lo_w = 2 * LANES
    assert r_decay + r_icl + r_vres <= lo_w and hgw == rww and d == 2 * hgw
    assert seq % (STEP_CHUNKS * CHUNK) == 0
    alpha = float((2 * depth) ** 0.25)

    lb_soft = jax.nn.softmax(hg_lower_bounds.astype(F32), axis=0)
    lower_bounds = jnp.cumsum(lb_soft, axis=0) - lb_soft[0]

    seg = jnp.asarray(np.tile(_hgrn_segment_matrix(CHUNK), (1, 2)), BF16)
    code = jnp.asarray(_split_code(CHUNK), jnp.int32)
    tri = jnp.asarray(np.tile(np.tril(np.ones((CHUNK, CHUNK), np.float32)), (1, 2)), BF16)
    bd = jnp.asarray(_block_ones(LANES, RW_HEAD), BF16)

    c_pad = jnp.zeros((SUBLANES, d), F32).at[:batch].set(c)
    cond = _ada_call(c_pad, w_ada, b_ada)

    x2 = x.reshape(batch * seq, d)
    tm = min(ROW_TILE, seq)
    v_first = None
    for l in range(depth):
        shift = cond[l, :batch, 0:d].reshape(batch, 1, d)
        scale = cond[l, :batch, d:2 * d].reshape(batch, 1, d)
        gate = cond[l, :batch, 2 * d:3 * d].reshape(batch, 1, d)

        wl = w_in[l]
        w_q, w_f, w_i, w_z = (wl[:, j * hgw:(j + 1) * hgw] for j in range(4))
        w_rw = wl[:, hg_cols:hg_cols + rw_main]
        w_lo = wl[:, hg_cols + rw_main:hg_cols + rw_main + r_decay + r_icl]
        w_gate = wl[:, hg_cols + rw_main + r_decay + r_icl:]
        w_vd = rw_v_down[l - 1] if l > 0 else jnp.zeros((d, r_vres), F32)
        pad = jnp.zeros((d, lo_w - (r_decay + r_icl + r_vres)), F32)
        w_f32 = jnp.concatenate([w_q, w_f, w_lo, w_vd, pad], axis=1).astype(BF16)
        w_b16 = jnp.concatenate([w_i, w_z, w_gate, w_rw], axis=1).astype(BF16)
        assert w_b16.shape[1] % (2 * LANES) == 0
        proj_f, proj_b = _proj_call(x2, shift, scale, w_f32, w_b16, seq, min(IN_ROW_TILE, seq), w_b16.shape[1] // 4)

        pf3 = proj_f.reshape(batch, seq, w_f32.shape[1])
        pb3 = proj_b.reshape(batch, seq, w_b16.shape[1])
        o_hg = _hgrn_call(pf3, pb3, lower_bounds[l].reshape(1, hgw), hg_norm_g[l].reshape(1, hgw), seg, code)

        mu_l = rw_mu[l]
        mu4 = jnp.zeros((SUBLANES, rww), F32).at[0:4].set(mu_l[:rw_main].reshape(4, rww))
        mu_lo = jnp.zeros((1, lo_w), F32).at[0, :r_decay + r_icl].set(mu_l[rw_main:])
        v0 = rw_v0[l - 1] if l > 0 else jnp.zeros((rww,), F32)
        par = jnp.stack([rw_w0[l], rw_a0[l], rw_k_k[l], rw_k_a[l], rw_r_k[l].reshape(rww),
                         rw_gn_g[l], rw_gn_b[l], v0], axis=0)
        wup = jnp.zeros((lo_w, rww), F32).at[0:r_decay].set(rw_w_up[l]).astype(BF16)
        aup = jnp.zeros((lo_w, rww), F32).at[r_decay:r_decay + r_icl].set(rw_a_up[l]).astype(BF16)
        vup_l = rw_v_up[l - 1] if l > 0 else jnp.zeros((r_vres, rww), F32)
        vup = jnp.zeros((lo_w, rww), F32).at[r_decay + r_icl:r_decay + r_icl + r_vres].set(vup_l).astype(BF16)
        col0 = (2 * hgw + w_gate.shape[1]) // rww
        res = _rwkv_call(pf3, pb3, mu4, mu_lo, par, wup, aup, vup, bd, tri, code, v_first, col0, 2 * hgw // lo_w)
        if l == 0:
            o_rw, v_first = res
        else:
            o_rw = res

        x2 = _out_call(x2, o_hg.reshape(batch * seq, hgw), o_rw.reshape(batch * seq, rww), proj_b, gate,
                       w_branch_hg[l].astype(BF16), w_branch_rw[l].astype(BF16), w_out[l].astype(BF16),
                       ln_g[l].reshape(1, d), ln_b[l].reshape(1, d), seq, tm, 2 * hgw // d, alpha)
    return x2.reshape(batch, seq, d)
```
